```python
import jax, jax.numpy as jnp
from jax import lax
import numpy as np

D_MODEL = 1024
BATCH = 8
SEQ = 2048
DEPTH = 4

MIX_WIDTH = D_MODEL
POOL_WIDTH = MIX_WIDTH // 2
POOL_WINDOWS = (2, 4, 8, 16)
N_POOL_GROUPS = len(POOL_WINDOWS)
POOL_GROUP = POOL_WIDTH // N_POOL_GROUPS
ATTN_WIDTH = MIX_WIDTH - POOL_WIDTH
HEAD_DIM = 64
N_HEADS = ATTN_WIDTH // HEAD_DIM
N_KV_HEADS = 2
GQA_GROUP = N_HEADS // N_KV_HEADS
WINDOW = 128
BLOCK = 128
Q_OFF = POOL_WIDTH
K_OFF = Q_OFF + N_HEADS * HEAD_DIM
V_OFF = K_OFF + N_KV_HEADS * HEAD_DIM
IN_WIDTH = V_OFF + N_KV_HEADS * HEAD_DIM

N_KEYS = 128
N_EXPERTS = N_KEYS * N_KEYS
PEER_HEADS = 8
PEER_TOPK = 16
D_KEY = 256
D_HALF = D_KEY // 2
PEER_CHUNK = 128

RMS_EPS = 1e-6
N_MOD = 6

kernel_name = "hybrid_pool_swa_peer_adaln_trunk"


def rms_norm(x, g):
    xf = x.astype(jnp.float32)
    y = xf * lax.rsqrt(jnp.mean(xf * xf, axis=-1, keepdims=True) + RMS_EPS)
    return (y * g.astype(jnp.float32)).astype(x.dtype)


def alibi_slopes(n_heads):
    return jnp.exp2(-8.0 * jnp.arange(1, n_heads + 1, dtype=jnp.float32) / n_heads)


def multiscale_pool(p, pool_w, pool_scale):
    B, S, _ = p.shape
    pg = p.reshape(B, S, N_POOL_GROUPS, POOL_GROUP)
    cs = jnp.cumsum(pg.astype(jnp.float32), axis=1)
    pos1 = jnp.arange(1, S + 1, dtype=jnp.float32)
    means = []
    for g, w in enumerate(POOL_WINDOWS):
        cg = cs[:, :, g]
        lag = jnp.pad(cg, ((0, 0), (w, 0), (0, 0)))[:, :S]
        means.append((cg - lag) / jnp.minimum(pos1, float(w))[None, :, None])
    pooled = jnp.stack(means, axis=2).astype(p.dtype) - pg
    mixed = jnp.einsum('bsgc,gcd->bsgd', pooled, pool_w)
    return mixed.reshape(B, S, POOL_WIDTH) * pool_scale


def band_blocks(t, nb):
    B = t.shape[0]
    tb = t.reshape(B, nb, BLOCK, N_KV_HEADS, HEAD_DIM)
    prev = jnp.pad(tb, ((0, 0), (1, 0), (0, 0), (0, 0), (0, 0)))[:, :nb]
    return jnp.concatenate([prev, tb], axis=2)


def sliding_window_attention(q, k, v, sinks):
    B, S = q.shape[:2]
    nb = S // BLOCK
    qb = q.reshape(B, nb, BLOCK, N_KV_HEADS, GQA_GROUP, HEAD_DIM)
    kb = band_blocks(k, nb)
    vb = band_blocks(v, nb)
    scores = jnp.einsum('bnqkgd,bnskd->bnkgqs', qb, kb,
                        preferred_element_type=jnp.float32) * (HEAD_DIM ** -0.5)
    qi = jnp.arange(BLOCK)[:, None]
    sj = jnp.arange(2 * BLOCK)[None, :]
    dist = qi + BLOCK - sj
    band_ok = (dist >= 0) & (dist < WINDOW)
    blk = jnp.arange(nb)[:, None, None]
    valid = band_ok[None] & ((blk > 0) | (sj[None] >= BLOCK))
    slopes = alibi_slopes(N_HEADS).reshape(N_KV_HEADS, GQA_GROUP)
    bias = -slopes[:, :, None, None] * dist.astype(jnp.float32)
    scores = jnp.where(valid[None, :, None, None], scores + bias[None, None], -jnp.inf)
    sink = sinks.astype(jnp.float32).reshape(N_KV_HEADS, GQA_GROUP)[None, None, :, :, None, None]
    sink = jnp.broadcast_to(sink, scores.shape[:-1] + (1,))
    probs = jax.nn.softmax(jnp.concatenate([scores, sink], axis=-1), axis=-1)[..., :-1]
    out = jnp.einsum('bnkgqs,bnskd->bnqkgd', probs.astype(v.dtype), vb)
    return out.reshape(B, S, N_HEADS * HEAD_DIM)


def peer_ffn(h, wq, subkeys, u, v):
    B, S, D = h.shape
    T = B * S
    hf = h.reshape(T, D)
    q = (hf @ wq).reshape(T, PEER_HEADS, 2, D_HALF)
    s = jnp.einsum('thpd,pnd->thpn', q, subkeys, preferred_element_type=jnp.float32)
    top_s, top_i = lax.top_k(s, PEER_TOPK)
    cand = top_s[:, :, 0, :, None] + top_s[:, :, 1, None, :]
    best_s, best_j = lax.top_k(cand.reshape(T, PEER_HEADS, PEER_TOPK * PEER_TOPK), PEER_TOPK)
    i1 = jnp.take_along_axis(top_i[:, :, 0], best_j // PEER_TOPK, axis=-1)
    i2 = jnp.take_along_axis(top_i[:, :, 1], best_j % PEER_TOPK, axis=-1)
    expert = (i1 * N_KEYS + i2).reshape(T, PEER_HEADS * PEER_TOPK)
    gate = jax.nn.softmax(best_s, axis=-1).reshape(T, PEER_HEADS * PEER_TOPK).astype(h.dtype)
    nc = T // PEER_CHUNK

    def expert_block(args):
        hc, ec, gc = args
        uc = jnp.take(u, ec, axis=0)
        a = jnp.einsum('cd,ced->ce', hc, uc)
        act = jax.nn.gelu(a, approximate=False) * gc
        vc = jnp.take(v, ec, axis=0)
        return jnp.einsum('ce,ced->cd', act, vc)

    out = lax.map(expert_block, (hf.reshape(nc, PEER_CHUNK, D),
                                 expert.reshape(nc, PEER_CHUNK, -1),
                                 gate.reshape(nc, PEER_CHUNK, -1)))
    return out.reshape(B, S, D)


def setup_inputs(seed: int = 0) -> dict:
    key = jax.random.key(seed)
    ks = jax.random.split(key, 20)
    f32 = jnp.float32
    nrm = lambda k, shape, s: jax.random.normal(k, shape, f32) * s
    return {
        "x": nrm(ks[0], (BATCH, SEQ, D_MODEL), 1.0),
        "c": nrm(ks[1], (BATCH, D_MODEL), 1.0),
        "w_ada": nrm(ks[2], (DEPTH, D_MODEL, N_MOD * D_MODEL), 0.5 * D_MODEL ** -0.5),
        "b_ada": nrm(ks[3], (DEPTH, N_MOD * D_MODEL), 0.02),
        "norm1_g": 1.0 + nrm(ks[4], (DEPTH, D_MODEL), 0.05),
        "norm2_g": 1.0 + nrm(ks[5], (DEPTH, D_MODEL), 0.05),
        "w_in": nrm(ks[6], (DEPTH, D_MODEL, IN_WIDTH), D_MODEL ** -0.5),
        "pool_w": nrm(ks[7], (DEPTH, N_POOL_GROUPS, POOL_GROUP, POOL_GROUP), POOL_GROUP ** -0.5),
        "pool_scale": 1.0 + nrm(ks[8], (DEPTH, POOL_WIDTH), 0.1),
        "q_norm_g": 1.0 + nrm(ks[9], (DEPTH, HEAD_DIM), 0.05),
        "k_norm_g": 1.0 + nrm(ks[10], (DEPTH, HEAD_DIM), 0.05),
        "attn_sinks": nrm(ks[11], (DEPTH, N_HEADS), 0.5),
        "mix_norm_g": 1.0 + nrm(ks[12], (DEPTH, MIX_WIDTH), 0.05),
        "w_out": nrm(ks[13], (DEPTH, MIX_WIDTH, D_MODEL), MIX_WIDTH ** -0.5),
        "peer_wq": nrm(ks[14], (DEPTH, D_MODEL, PEER_HEADS * D_KEY), D_MODEL ** -0.5),
        "peer_subkeys": nrm(ks[15], (DEPTH, 2, N_KEYS, D_HALF), D_HALF ** -0.5),
        "peer_u": nrm(ks[16], (DEPTH, N_EXPERTS, D_MODEL), D_MODEL ** -0.5),
        "peer_v": nrm(ks[17], (DEPTH, N_EXPERTS, D_MODEL), PEER_HEADS ** -0.5),
    }


def reference(x, c, w_ada, b_ada, norm1_g, norm2_g, w_in, pool_w, pool_scale, q_norm_g,
              k_norm_g, attn_sinks, mix_norm_g, w_out, peer_wq, peer_subkeys, peer_u, peer_v):
    B, S, D = x.shape
    cond = jax.nn.silu(c)
    for l in range(DEPTH):
        mod = (cond @ w_ada[l] + b_ada[l])[:, None, :]
        sh1, sc1, g1, sh2, sc2, g2 = jnp.split(mod, N_MOD, axis=-1)

        h = rms_norm(x, norm1_g[l]) * (1.0 + sc1) + sh1
        proj = h @ w_in[l]
        p = proj[..., :Q_OFF]
        q = proj[..., Q_OFF:K_OFF].reshape(B, S, N_HEADS, HEAD_DIM)
        k = proj[..., K_OFF:V_OFF].reshape(B, S, N_KV_HEADS, HEAD_DIM)
        v = proj[..., V_OFF:].reshape(B, S, N_KV_HEADS, HEAD_DIM)
        q = rms_norm(q, q_norm_g[l])
        k = rms_norm(k, k_norm_g[l])
        pool_out = multiscale_pool(p, pool_w[l], pool_scale[l])
        attn_out = sliding_window_attention(q, k, v, attn_sinks[l])
        y = jnp.concatenate([pool_out, attn_out], axis=-1).reshape(B, S, 2, MIX_WIDTH // 2)
        y = rms_norm(y, jnp.ones((MIX_WIDTH // 2,), jnp.float32)).reshape(B, S, MIX_WIDTH) * mix_norm_g[l]
        x = x + g1 * (y @ w_out[l])

        h2 = rms_norm(x, norm2_g[l]) * (1.0 + sc2) + sh2
        x = x + g2 * peer_ffn(h2, peer_wq[l], peer_subkeys[l], peer_u[l], peer_v[l])
    return x
```

```python
import functools

import jax
import jax.numpy as jnp
from jax import lax
from jax.experimental import pallas as pl
from jax.experimental.pallas import tpu as pltpu

F32 = jnp.float32
BF16 = jnp.bfloat16

D_MODEL = 1024
POOL_WIDTH = 512
POOL_WINDOWS = (2, 4, 8, 16)
POOL_GROUP = 128
POOL_HALO = 16
ATTN_WIDTH = 512
HEAD_DIM = 64
N_HEADS = 8
N_KV_HEADS = 2
KV_WIDTH = N_KV_HEADS * HEAD_DIM
ATT_BLOCK = 128
N_KEYS = 128
N_EXPERTS = N_KEYS * N_KEYS
PEER_HEADS = 8
PEER_TOPK = 16
D_HALF = 128
RMS_EPS = 1e-6
N_MOD = 6
NEG = -1e30

LANES = 128
VMEM_LIMIT = 56 * 1024 * 1024


def _cparams(sem):
    return pltpu.CompilerParams(dimension_semantics=sem, vmem_limit_bytes=VMEM_LIMIT)


def _mod_kernel(c_ref, w_ref, b_ref, o_ref):
    c = c_ref[...]
    cond = c * jax.nn.sigmoid(c)
    o_ref[0] = jnp.dot(cond, w_ref[0], preferred_element_type=F32,
                       precision=lax.Precision.HIGHEST) + b_ref[0]


def _adaln_mod(c, w_ada, b_ada, tn=1536):
    depth, d, n = w_ada.shape
    b = c.shape[0]
    return pl.pallas_call(
        _mod_kernel,
        grid=(depth, n // tn),
        in_specs=[
            pl.BlockSpec((b, d), lambda l, j: (0, 0)),
            pl.BlockSpec((1, d, tn), lambda l, j: (l, 0, j)),
            pl.BlockSpec((1, 1, tn), lambda l, j: (l, 0, j)),
        ],
        out_specs=pl.BlockSpec((1, b, tn), lambda l, j: (l, 0, j)),
        out_shape=jax.ShapeDtypeStruct((depth, b, n), F32),
        compiler_params=_cparams(("parallel", "parallel")),
        name="adaln_mod",
    )(c, w_ada, b_ada.reshape(depth, 1, n))


def _split_dot(t, bd):
    hi = t.astype(BF16)
    lo = (t - hi.astype(F32)).astype(BF16)
    return (jnp.dot(hi, bd, preferred_element_type=F32)
            + jnp.dot(lo, bd, preferred_element_type=F32))


def _inproj_kernel(x_ref, mod_ref, g_ref, w_ref, qg_ref, kg_ref, bd_ref,
                   p_ref, q_ref, k_ref, v_ref):
    x = x_ref[0]
    ms = jnp.mean(x * x, axis=-1, keepdims=True)
    y = x * lax.rsqrt(ms + RMS_EPS) * g_ref[...]
    h = y * (1.0 + mod_ref[0, 1:2, :]) + mod_ref[0, 0:1, :]
    proj = jnp.dot(h.astype(BF16), w_ref[...], preferred_element_type=F32)
    q0, k0, v0 = POOL_WIDTH, POOL_WIDTH + ATTN_WIDTH, POOL_WIDTH + ATTN_WIDTH + KV_WIDTH
    p_ref[0] = proj[:, :q0]
    q = proj[:, q0:k0]
    k = proj[:, k0:v0]
    bd = bd_ref[...]
    q_ms = _split_dot(q * q, bd)
    k_ms = _split_dot(k * k, bd[:KV_WIDTH, :KV_WIDTH])
    qn = q * lax.rsqrt(q_ms + RMS_EPS) * qg_ref[...]
    kn = k * lax.rsqrt(k_ms + RMS_EPS) * kg_ref[...]
    q_ref[0] = (qn * (HEAD_DIM ** -0.5)).astype(BF16)
    k_ref[0] = kn.astype(BF16)
    v_ref[0] = proj[:, v0:].astype(BF16)


def _inproj(x, mod_l, norm_g, w_in_b, qg_t, kg_t, bd, ts):
    b, s, d = x.shape
    n_in = w_in_b.shape[1]
    full = lambda shape: pl.BlockSpec(shape, lambda bi, i: (0,) * len(shape))
    return pl.pallas_call(
        _inproj_kernel,
        grid=(b, s // ts),
        in_specs=[
            pl.BlockSpec((1, ts, d), lambda bi, i: (bi, i, 0)),
            pl.BlockSpec((1, N_MOD, d), lambda bi, i: (bi, 0, 0)),
            full((1, d)),
            full((d, n_in)),
            full((1, ATTN_WIDTH)),
            full((1, KV_WIDTH)),
            full((ATTN_WIDTH, ATTN_WIDTH)),
        ],
        out_specs=[
            pl.BlockSpec((1, ts, POOL_WIDTH), lambda bi, i: (bi, i, 0)),
            pl.BlockSpec((1, ts, ATTN_WIDTH), lambda bi, i: (bi, i, 0)),
            pl.BlockSpec((1, ts, KV_WIDTH), lambda bi, i: (bi, i, 0)),
            pl.BlockSpec((1, ts, KV_WIDTH), lambda bi, i: (bi, i, 0)),
        ],
        out_shape=[
            jax.ShapeDtypeStruct((b, s, POOL_WIDTH), F32),
            jax.ShapeDtypeStruct((b, s, ATTN_WIDTH), BF16),
            jax.ShapeDtypeStruct((b, s, KV_WIDTH), BF16),
            jax.ShapeDtypeStruct((b, s, KV_WIDTH), BF16),
        ],
        compiler_params=_cparams(("parallel", "parallel")),
        name="inproj",
    )(x, mod_l, norm_g, w_in_b, qg_t, kg_t, bd)


def _mixer_kernel(sink_ref, x_ref, p_ref, pp_ref, q_ref, k_ref, kp_ref, v_ref, vp_ref,
                  mod_ref, bias_ref, pw_ref, ps_ref, mg_ref, wo_ref, o_ref,
                  kext_ref, vext_ref, attn_ref, pool_ref, *, ts):
    i = pl.program_id(1)
    nblk = ts // ATT_BLOCK

    has_prev = (i > 0).astype(F32)
    pext = jnp.concatenate([pp_ref[0] * has_prev, p_ref[0]], axis=0)
    pos1 = (lax.broadcasted_iota(jnp.int32, (ts, 1), 0) + i * ts + 1).astype(F32)
    for g, w in enumerate(POOL_WINDOWS):
        xg = pext[:, g * POOL_GROUP:(g + 1) * POOL_GROUP]
        sg = xg
        step = 1
        while step < w:
            sg = sg + pltpu.roll(sg, step, 0)
            step *= 2
        pooled = sg[POOL_HALO:] / jnp.minimum(pos1, float(w)) - xg[POOL_HALO:]
        mixed = jnp.dot(pooled.astype(BF16), pw_ref[g], preferred_element_type=F32)
        pool_ref[:, g * POOL_GROUP:(g + 1) * POOL_GROUP] = (
            mixed * ps_ref[:, g * POOL_GROUP:(g + 1) * POOL_GROUP])

    kext_ref[0:ATT_BLOCK] = kp_ref[0]
    kext_ref[ATT_BLOCK:] = k_ref[0]
    vext_ref[0:ATT_BLOCK] = vp_ref[0]
    vext_ref[ATT_BLOCK:] = v_ref[0]
    lane = lax.broadcasted_iota(jnp.int32, (ATT_BLOCK, LANES), 1)
    low_half = lane < HEAD_DIM
    col = lax.broadcasted_iota(jnp.int32, (1, 2 * ATT_BLOCK), 1)
    zero_b = jnp.zeros((ATT_BLOCK, LANES), BF16)
    for j in range(nblk):
        kb = kext_ref[j * ATT_BLOCK:(j + 2) * ATT_BLOCK]
        vb = vext_ref[j * ATT_BLOCK:(j + 2) * ATT_BLOCK]
        qs = q_ref[0, j * ATT_BLOCK:(j + 1) * ATT_BLOCK, :]
        stack = []
        for r in range(N_HEADS):
            slab = qs[:, (r % 4) * LANES:(r % 4 + 1) * LANES]
            keep = low_half if r < 4 else jnp.logical_not(low_half)
            stack.append(jnp.where(keep, slab, zero_b))
        qst = jnp.concatenate(stack, axis=0)
        scores = lax.dot_general(qst, kb, (((1,), (1,)), ((), ())),
                                 preferred_element_type=F32)
        first = jnp.logical_and(i == 0, j == 0)
        pen = jnp.where(jnp.logical_and(first, col < ATT_BLOCK), NEG, 0.0)
        probs = []
        inv = []
        for r in range(N_HEADS):
            sr = scores[r * ATT_BLOCK:(r + 1) * ATT_BLOCK] + bias_ref[r] + pen
            sink = sink_ref[r]
            m = jnp.maximum(jnp.max(sr, axis=-1, keepdims=True), sink)
            e = jnp.exp(sr - m)
            den = jnp.sum(e, axis=-1, keepdims=True) + jnp.exp(sink - m)
            probs.append(e.astype(BF16))
            inv.append(1.0 / den)
        pst = jnp.concatenate(probs, axis=0)
        pv = jnp.dot(pst, vb, preferred_element_type=F32)
        for m4 in range(4):
            lo = pv[m4 * ATT_BLOCK:(m4 + 1) * ATT_BLOCK] * inv[m4]
            hi = pv[(m4 + 4) * ATT_BLOCK:(m4 + 5) * ATT_BLOCK] * inv[m4 + 4]
            attn_ref[j * ATT_BLOCK:(j + 1) * ATT_BLOCK, m4 * LANES:(m4 + 1) * LANES] = (
                jnp.where(low_half, lo, hi))

    ya = pool_ref[...]
    yb = attn_ref[...]
    ya = ya * lax.rsqrt(jnp.mean(ya * ya, axis=-1, keepdims=True) + RMS_EPS) * mg_ref[:, :POOL_WIDTH]
    yb = yb * lax.rsqrt(jnp.mean(yb * yb, axis=-1, keepdims=True) + RMS_EPS) * mg_ref[:, POOL_WIDTH:]
    out = (jnp.dot(ya.astype(BF16), wo_ref[:POOL_WIDTH, :], preferred_element_type=F32)
           + jnp.dot(yb.astype(BF16), wo_ref[POOL_WIDTH:, :], preferred_element_type=F32))
    o_ref[0] = x_ref[0] + mod_ref[0, 2:3, :] * out


def _mixer(x, p, q, k, v, mod_l, sinks, bias, pool_w_b, pool_scale, mix_g, w_out_b, ts):
    b, s, d = x.shape
    nb = ts // ATT_BLOCK
    nh = ts // POOL_HALO
    full = lambda shape: pl.BlockSpec(shape, lambda bi, i: (0,) * len(shape))
    row = lambda w: pl.BlockSpec((1, ts, w), lambda bi, i: (bi, i, 0))
    prev_blk = pl.BlockSpec((1, ATT_BLOCK, KV_WIDTH),
                            lambda bi, i: (bi, jnp.maximum(i * nb - 1, 0), 0))
    return pl.pallas_call(
        functools.partial(_mixer_kernel, ts=ts),
        grid=(b, s // ts),
        in_specs=[
            pl.BlockSpec(memory_space=pltpu.SMEM),
            row(d),
            row(POOL_WIDTH),
            pl.BlockSpec((1, POOL_HALO, POOL_WIDTH),
                         lambda bi, i: (bi, jnp.maximum(i * nh - 1, 0), 0)),
            row(ATTN_WIDTH),
            row(KV_WIDTH), prev_blk,
            row(KV_WIDTH), prev_blk,
            pl.BlockSpec((1, N_MOD, d), lambda bi, i: (bi, 0, 0)),
            full((N_HEADS, ATT_BLOCK, 2 * ATT_BLOCK)),
            full((len(POOL_WINDOWS), POOL_GROUP, POOL_GROUP)),
            full((1, POOL_WIDTH)),
            full((1, d)),
            full((d, d)),
        ],
        out_specs=row(d),
        out_shape=jax.ShapeDtypeStruct((b, s, d), F32),
        scratch_shapes=[
            pltpu.VMEM((ts + ATT_BLOCK, KV_WIDTH), BF16),
            pltpu.VMEM((ts + ATT_BLOCK, KV_WIDTH), BF16),
            pltpu.VMEM((ts, ATTN_WIDTH), F32),
            pltpu.VMEM((ts, POOL_WIDTH), F32),
        ],
        compiler_params=_cparams(("parallel", "parallel")),
        name="mixer",
    )(sinks, x, p, p, q, k, k, v, v, mod_l, bias, pool_w_b, pool_scale, mix_g, w_out_b)


def _extract_top(s, want_rank):
    cur = s
    tops = []
    rank = jnp.full(s.shape, float(PEER_TOPK), F32) if want_rank else None
    for r in range(PEER_TOPK):
        m = jnp.max(cur, axis=0, keepdims=True)
        sel = cur == m
        tops.append(m)
        if want_rank:
            rank = jnp.where(sel, float(r), rank)
        if r + 1 < PEER_TOPK:
            cur = jnp.where(sel, NEG, cur)
    return tops, rank


def _route_kernel(x_ref, mod_ref, g_ref, wq_ref, keys_ref,
                  h_ref, r2_ref, e2_ref, n1_ref, w1_ref, s_ref, *, tt):
    x = x_ref[...]
    ms = jnp.mean(x * x, axis=-1, keepdims=True)
    y = x * lax.rsqrt(ms + RMS_EPS) * g_ref[...]
    h = (y * (1.0 + mod_ref[0, 4:5, :]) + mod_ref[0, 3:4, :]).astype(BF16)
    h_ref[...] = h
    q = jnp.dot(h, wq_ref[...], preferred_element_type=F32).astype(BF16)
    for hp in range(2 * PEER_HEADS):
        qhp = q[:, hp * D_HALF:(hp + 1) * D_HALF]
        s_ref[hp] = lax.dot_general(keys_ref[hp % 2], qhp, (((1,), (1,)), ((), ())),
                                    preferred_element_type=F32)

    pairs = [(r1, r2) for r1 in range(PEER_TOPK) for r2 in range(PEER_TOPK)
             if (r1 + 1) * (r2 + 1) <= PEER_TOPK]

    def per_chunk(idx, carry):
        head = idx // (tt // LANES)
        c0 = pl.multiple_of((idx % (tt // LANES)) * LANES, LANES)
        s1 = s_ref[2 * head, :, pl.ds(c0, LANES)]
        s2 = s_ref[2 * head + 1, :, pl.ds(c0, LANES)]
        a, _ = _extract_top(s1, False)
        bt, rank2 = _extract_top(s2, True)
        cand = jnp.concatenate([a[r1] + bt[r2] for r1, r2 in pairs], axis=0)
        for _ in range(PEER_TOPK - 1):
            m = jnp.max(cand, axis=0, keepdims=True)
            cand = jnp.where(cand == m, NEG, cand)
        thr = jnp.max(cand, axis=0, keepdims=True)
        amat = jnp.concatenate(a, axis=0)
        top = a[0] + bt[0]
        cnt = jnp.zeros_like(amat)
        z = jnp.zeros_like(amat)
        for r2 in range(PEER_TOPK):
            t = amat + bt[r2]
            hit = t >= thr
            cnt = cnt + jnp.where(hit, 1.0, 0.0)
            z = z + jnp.where(hit, jnp.exp(t - top), 0.0)
        zinv = 1.0 / jnp.sum(z, axis=0, keepdims=True)
        n1 = jnp.zeros_like(s1)
        for r1 in range(PEER_TOPK):
            n1 = jnp.where(s1 == a[r1], cnt[r1:r1 + 1, :], n1)
        r2_ref[head, :, pl.ds(c0, LANES)] = rank2.astype(BF16)
        e2_ref[head, :, pl.ds(c0, LANES)] = jnp.exp(s2 - bt[0]).astype(BF16)
        n1_ref[head, :, pl.ds(c0, LANES)] = n1
        w1_ref[head, :, pl.ds(c0, LANES)] = jnp.exp(s1 - a[0]) * zinv
        return carry

    lax.fori_loop(0, PEER_HEADS * (tt // LANES), per_chunk, 0)


def _route(x2, mod_l, norm_g, wq_b, keys_b, tt, s_len):
    t, d = x2.shape
    per_b = s_len // tt
    full = lambda shape: pl.BlockSpec(shape, lambda i: (0,) * len(shape))
    stat = lambda: pl.BlockSpec((PEER_HEADS, N_KEYS, tt), lambda i: (0, 0, i))
    return pl.pallas_call(
        functools.partial(_route_kernel, tt=tt),
        grid=(t // tt,),
        in_specs=[
            pl.BlockSpec((tt, d), lambda i: (i, 0)),
            pl.BlockSpec((1, N_MOD, d), lambda i: (i // per_b, 0, 0)),
            full((1, d)),
            full(wq_b.shape),
            full(keys_b.shape),
        ],
        out_specs=[pl.BlockSpec((tt, d), lambda i: (i, 0)), stat(), stat(), stat(), stat()],
        out_shape=[
            jax.ShapeDtypeStruct((t, d), BF16),
            jax.ShapeDtypeStruct((PEER_HEADS, N_KEYS, t), BF16),
            jax.ShapeDtypeStruct((PEER_HEADS, N_KEYS, t), BF16),
            jax.ShapeDtypeStruct((PEER_HEADS, N_KEYS, t), F32),
            jax.ShapeDtypeStruct((PEER_HEADS, N_KEYS, t), F32),
        ],
        scratch_shapes=[pltpu.VMEM((2 * PEER_HEADS, N_KEYS, tt), F32)],
        compiler_params=_cparams(("parallel",)),
        name="peer_route",
    )(x2, mod_l, norm_g, wq_b, keys_b)


def _peer_kernel(x_ref, h_ref, mod_ref, r2_ref, e2_ref, n1_ref, w1_ref, u_ref, vt_ref,
                 o_ref, acc_ref, act_ref, *, n_sub):
    e = pl.program_id(1)

    @pl.when(e == 0)
    def _():
        acc_ref[...] = jnp.zeros_like(acc_ref)

    a = lax.dot_general(u_ref[...], h_ref[...], (((1,), (1,)), ((), ())),
                        preferred_element_type=F32)
    for j in range(n_sub):
        gate = None
        for hd in range(PEER_HEADS):
            n1 = n1_ref[hd, j:j + 1, :].astype(BF16)
            w1 = w1_ref[hd, j:j + 1, :].astype(BF16)
            term = jnp.where(r2_ref[hd] < n1, e2_ref[hd] * w1, jnp.zeros((), BF16))
            gate = term if gate is None else gate + term
        aj = a[j * N_KEYS:(j + 1) * N_KEYS, :]
        gelu = 0.5 * aj * (1.0 + lax.erf(aj * (2.0 ** -0.5)))
        act_ref[j * N_KEYS:(j + 1) * N_KEYS, :] = (gelu * gate.astype(F32)).astype(BF16)
    acc_ref[...] += jnp.dot(vt_ref[...], act_ref[...], preferred_element_type=F32)

    @pl.when(e == pl.num_programs(1) - 1)
    def _():
        o_ref[...] = x_ref[...] + mod_ref[0, 5:6, :] * acc_ref[...].T


def _peer(x2, h2, mod_l, r2, e2, n1, w1, u_b, vt_b, tt, e_blk, s_len):
    t, d = x2.shape
    n_sub = e_blk // N_KEYS
    per_b = s_len // tt
    stat = lambda: pl.BlockSpec((PEER_HEADS, N_KEYS, tt), lambda i, e: (0, 0, i))
    rows = lambda: pl.BlockSpec((PEER_HEADS, n_sub, tt), lambda i, e: (0, e, i))
    return pl.pallas_call(
        functools.partial(_peer_kernel, n_sub=n_sub),
        grid=(t // tt, N_EXPERTS // e_blk),
        in_specs=[
            pl.BlockSpec((tt, d), lambda i, e: (i, 0)),
            pl.BlockSpec((tt, d), lambda i, e: (i, 0)),
            pl.BlockSpec((1, N_MOD, d), lambda i, e: (i // per_b, 0, 0)),
            stat(), stat(), rows(), rows(),
            pl.BlockSpec((e_blk, d), lambda i, e: (e, 0)),
            pl.BlockSpec((d, e_blk), lambda i, e: (0, e)),
        ],
        out_specs=pl.BlockSpec((tt, d), lambda i, e: (i, 0)),
        out_shape=jax.ShapeDtypeStruct((t, d), F32),
        scratch_shapes=[pltpu.VMEM((d, tt), F32), pltpu.VMEM((e_blk, tt), BF16)],
        compiler_params=_cparams(("parallel", "arbitrary")),
        name="peer_dense",
    )(x2, h2, mod_l, r2, e2, n1, w1, u_b, vt_b)


def _head_perm():
    idx = []
    for m in range(4):
        for half in range(2):
            hd = m + 4 * half
            idx.extend(range(hd * HEAD_DIM, (hd + 1) * HEAD_DIM))
    return jnp.asarray(idx, jnp.int32)


def _attn_bias():
    qi = jnp.arange(ATT_BLOCK)[:, None]
    sj = jnp.arange(2 * ATT_BLOCK)[None, :]
    dist = qi + ATT_BLOCK - sj
    ok = (dist >= 0) & (dist < ATT_BLOCK)
    slopes = jnp.exp2(-8.0 * jnp.arange(1, N_HEADS + 1, dtype=F32) / N_HEADS)
    bias = -slopes[:, None, None] * dist.astype(F32)[None]
    return jnp.where(ok[None], bias, NEG)


def kernel(x, c, w_ada, b_ada, norm1_g, norm2_g, w_in, pool_w, pool_scale, q_norm_g, k_norm_g,
           attn_sinks, mix_norm_g, w_out, peer_wq, peer_subkeys, peer_u, peer_v):
    b, s, d = x.shape
    depth = w_ada.shape[0]
    t = b * s
    ts = min(512, s)
    tt_route = min(512, s)
    tt_peer = min(1024, s)
    e_blk = 1024

    perm = _head_perm()
    q_cols = POOL_WIDTH + perm
    col_perm = jnp.concatenate([jnp.arange(POOL_WIDTH), q_cols,
                                jnp.arange(POOL_WIDTH + ATTN_WIDTH, w_in.shape[2])])
    row_perm = jnp.concatenate([jnp.arange(POOL_WIDTH), q_cols])
    w_in_b = w_in[:, :, col_perm].astype(BF16)
    w_out_b = w_out[:, row_perm, :].astype(BF16)
    mix_g = mix_norm_g[:, row_perm]
    pool_w_b = pool_w.astype(BF16)
    wq_b = peer_wq.astype(BF16)
    keys_b = peer_subkeys.astype(BF16)
    u_b = peer_u.astype(BF16)
    vt_b = jnp.swapaxes(peer_v, 1, 2).astype(BF16)
    qg_t = jnp.tile(q_norm_g, (1, N_HEADS))
    kg_t = jnp.tile(k_norm_g, (1, N_KV_HEADS))
    lane_head = jnp.arange(ATTN_WIDTH) // HEAD_DIM
    bd = jnp.where(lane_head[:, None] == lane_head[None, :], 1.0 / HEAD_DIM, 0.0).astype(BF16)
    bias = _attn_bias()

    mod = _adaln_mod(c, w_ada, b_ada).reshape(depth, b, N_MOD, d)

    for l in range(depth):
        p, q, k, v = _inproj(x, mod[l], norm1_g[l:l + 1], w_in_b[l], qg_t[l:l + 1], kg_t[l:l + 1], bd, ts)
        x = _mixer(x, p, q, k, v, mod[l], attn_sinks[l], bias, pool_w_b[l], pool_scale[l:l + 1],
                   mix_g[l:l + 1], w_out_b[l], ts)
        x2 = x.reshape(t, d)
        h2, r2, e2, n1, w1 = _route(x2, mod[l], norm2_g[l:l + 1], wq_b[l], keys_b[l], tt_route, s)
        x = _peer(x2, h2, mod[l], r2, e2, n1, w1, u_b[l], vt_b[l], tt_peer, e_blk, s).reshape(b, s, d)
    return x
```

```python
import functools

import jax
import jax.numpy as jnp
from jax import lax
from jax.experimental import pallas as pl
from jax.experimental.pallas import tpu as pltpu

F32 = jnp.float32
BF16 = jnp.bfloat16

D_MODEL = 1024
POOL_WIDTH = 512
POOL_WINDOWS = (2, 4, 8, 16)
POOL_GROUP = 128
POOL_HALO = 16
ATTN_WIDTH = 512
HEAD_DIM = 64
N_HEADS = 8
N_KV_HEADS = 2
KV_WIDTH = N_KV_HEADS * HEAD_DIM
ATT_BLOCK = 128
N_KEYS = 128
N_EXPERTS = N_KEYS * N_KEYS
PEER_HEADS = 8
PEER_TOPK = 16
D_HALF = 128
RMS_EPS = 1e-6
N_MOD = 6
NEG = -1e30

LANES = 128
VMEM_LIMIT = 56 * 1024 * 1024


def _cparams(sem):
    return pltpu.CompilerParams(dimension_semantics=sem, vmem_limit_bytes=VMEM_LIMIT)


def _mod_kernel(c_ref, w_ref, b_ref, o_ref):
    c = c_ref[...]
    cond = c * jax.nn.sigmoid(c)
    o_ref[0] = jnp.dot(cond, w_ref[0], preferred_element_type=F32,
                       precision=lax.Precision.HIGHEST) + b_ref[0]


def _adaln_mod(c, w_ada, b_ada, tn=1536):
    depth, d, n = w_ada.shape
    b = c.shape[0]
    return pl.pallas_call(
        _mod_kernel,
        grid=(depth, n // tn),
        in_specs=[
            pl.BlockSpec((b, d), lambda l, j: (0, 0)),
            pl.BlockSpec((1, d, tn), lambda l, j: (l, 0, j)),
            pl.BlockSpec((1, 1, tn), lambda l, j: (l, 0, j)),
        ],
        out_specs=pl.BlockSpec((1, b, tn), lambda l, j: (l, 0, j)),
        out_shape=jax.ShapeDtypeStruct((depth, b, n), F32),
        compiler_params=_cparams(("parallel", "parallel")),
        name="adaln_mod",
    )(c, w_ada, b_ada.reshape(depth, 1, n))


def _split_dot(t, bd):
    hi = t.astype(BF16)
    lo = (t - hi.astype(F32)).astype(BF16)
    return (jnp.dot(hi, bd, preferred_element_type=F32)
            + jnp.dot(lo, bd, preferred_element_type=F32))


def _inproj_kernel(x_ref, mod_ref, g_ref, w_ref, qg_ref, kg_ref, bd_ref,
                   p_ref, q_ref, k_ref, v_ref):
    x = x_ref[0]
    ms = jnp.mean(x * x, axis=-1, keepdims=True)
    y = x * lax.rsqrt(ms + RMS_EPS) * g_ref[...]
    h = y * (1.0 + mod_ref[0, 1:2, :]) + mod_ref[0, 0:1, :]
    proj = jnp.dot(h.astype(BF16), w_ref[...], preferred_element_type=F32)
    q0, k0, v0 = POOL_WIDTH, POOL_WIDTH + ATTN_WIDTH, POOL_WIDTH + ATTN_WIDTH + KV_WIDTH
    p_ref[0] = proj[:, :q0]
    q = proj[:, q0:k0]
    k = proj[:, k0:v0]
    bd = bd_ref[...]
    q_ms = _split_dot(q * q, bd)
    k_ms = _split_dot(k * k, bd[:KV_WIDTH, :KV_WIDTH])
    qn = q * lax.rsqrt(q_ms + RMS_EPS) * qg_ref[...]
    kn = k * lax.rsqrt(k_ms + RMS_EPS) * kg_ref[...]
    q_ref[0] = (qn * (HEAD_DIM ** -0.5)).astype(BF16)
    k_ref[0] = kn.astype(BF16)
    v_ref[0] = proj[:, v0:].astype(BF16)


def _inproj(x, mod_l, norm_g, w_in_b, qg_t, kg_t, bd, ts):
    b, s, d = x.shape
    n_in = w_in_b.shape[1]
    full = lambda shape: pl.BlockSpec(shape, lambda bi, i: (0,) * len(shape))
    return pl.pallas_call(
        _inproj_kernel,
        grid=(b, s // ts),
        in_specs=[
            pl.BlockSpec((1, ts, d), lambda bi, i: (bi, i, 0)),
            pl.BlockSpec((1, N_MOD, d), lambda bi, i: (bi, 0, 0)),
            full((1, d)),
            full((d, n_in)),
            full((1, ATTN_WIDTH)),
            full((1, KV_WIDTH)),
            full((ATTN_WIDTH, ATTN_WIDTH)),
        ],
        out_specs=[
            pl.BlockSpec((1, ts, POOL_WIDTH), lambda bi, i: (bi, i, 0)),
            pl.BlockSpec((1, ts, ATTN_WIDTH), lambda bi, i: (bi, i, 0)),
            pl.BlockSpec((1, ts, KV_WIDTH), lambda bi, i: (bi, i, 0)),
            pl.BlockSpec((1, ts, KV_WIDTH), lambda bi, i: (bi, i, 0)),
        ],
        out_shape=[
            jax.ShapeDtypeStruct((b, s, POOL_WIDTH), F32),
            jax.ShapeDtypeStruct((b, s, ATTN_WIDTH), BF16),
            jax.ShapeDtypeStruct((b, s, KV_WIDTH), BF16),
            jax.ShapeDtypeStruct((b, s, KV_WIDTH), BF16),
        ],
        compiler_params=_cparams(("parallel", "parallel")),
        name="inproj",
    )(x, mod_l, norm_g, w_in_b, qg_t, kg_t, bd)


def _mixer_kernel(sink_ref, x_ref, p_ref, pp_ref, q_ref, k_ref, kp_ref, v_ref, vp_ref,
                  mod_ref, bias_ref, pw_ref, ps_ref, mg_ref, wo_ref, o_ref,
                  kext_ref, vext_ref, attn_ref, pool_ref, *, ts):
    i = pl.program_id(1)
    nblk = ts // ATT_BLOCK

    has_prev = (i > 0).astype(F32)
    pext = jnp.concatenate([pp_ref[0] * has_prev, p_ref[0]], axis=0)
    pos1 = (lax.broadcasted_iota(jnp.int32, (ts, 1), 0) + i * ts + 1).astype(F32)
    for g, w in enumerate(POOL_WINDOWS):
        xg = pext[:, g * POOL_GROUP:(g + 1) * POOL_GROUP]
        sg = xg
        step = 1
        while step < w:
            sg = sg + pltpu.roll(sg, step, 0)
            step *= 2
        pooled = sg[POOL_HALO:] / jnp.minimum(pos1, float(w)) - xg[POOL_HALO:]
        mixed = jnp.dot(pooled.astype(BF16), pw_ref[g], preferred_element_type=F32)
        pool_ref[:, g * POOL_GROUP:(g + 1) * POOL_GROUP] = (
            mixed * ps_ref[:, g * POOL_GROUP:(g + 1) * POOL_GROUP])

    kext_ref[0:ATT_BLOCK] = kp_ref[0]
    kext_ref[ATT_BLOCK:] = k_ref[0]
    vext_ref[0:ATT_BLOCK] = vp_ref[0]
    vext_ref[ATT_BLOCK:] = v_ref[0]
    lane = lax.broadcasted_iota(jnp.int32, (ATT_BLOCK, LANES), 1)
    low_half = lane < HEAD_DIM
    col = lax.broadcasted_iota(jnp.int32, (1, 2 * ATT_BLOCK), 1)
    zero_b = jnp.zeros((ATT_BLOCK, LANES), BF16)
    for j in range(nblk):
        kb = kext_ref[j * ATT_BLOCK:(j + 2) * ATT_BLOCK]
        vb = vext_ref[j * ATT_BLOCK:(j + 2) * ATT_BLOCK]
        qs = q_ref[0, j * ATT_BLOCK:(j + 1) * ATT_BLOCK, :]
        stack = []
        for r in range(N_HEADS):
            slab = qs[:, (r % 4) * LANES:(r % 4 + 1) * LANES]
            keep = low_half if r < 4 else jnp.logical_not(low_half)
            stack.append(jnp.where(keep, slab, zero_b))
        qst = jnp.concatenate(stack, axis=0)
        scores = lax.dot_general(qst, kb, (((1,), (1,)), ((), ())),
                                 preferred_element_type=F32)
        first = jnp.logical_and(i == 0, j == 0)
        pen = jnp.where(jnp.logical_and(first, col < ATT_BLOCK), NEG, 0.0)
        probs = []
        inv = []
        for r in range(N_HEADS):
            sr = scores[r * ATT_BLOCK:(r + 1) * ATT_BLOCK] + bias_ref[r] + pen
            sink = sink_ref[r]
            m = jnp.maximum(jnp.max(sr, axis=-1, keepdims=True), sink)
            e = jnp.exp(sr - m)
            den = jnp.sum(e, axis=-1, keepdims=True) + jnp.exp(sink - m)
            probs.append(e.astype(BF16))
            inv.append(1.0 / den)
        pst = jnp.concatenate(probs, axis=0)
        pv = jnp.dot(pst, vb, preferred_element_type=F32)
        for m4 in range(4):
            lo = pv[m4 * ATT_BLOCK:(m4 + 1) * ATT_BLOCK] * inv[m4]
            hi = pv[(m4 + 4) * ATT_BLOCK:(m4 + 5) * ATT_BLOCK] * inv[m4 + 4]
            attn_ref[j * ATT_BLOCK:(j + 1) * ATT_BLOCK, m4 * LANES:(m4 + 1) * LANES] = (
                jnp.where(low_half, lo, hi))

    ya = pool_ref[...]
    yb = attn_ref[...]
    ya = ya * lax.rsqrt(jnp.mean(ya * ya, axis=-1, keepdims=True) + RMS_EPS) * mg_ref[:, :POOL_WIDTH]
    yb = yb * lax.rsqrt(jnp.mean(yb * yb, axis=-1, keepdims=True) + RMS_EPS) * mg_ref[:, POOL_WIDTH:]
    out = (jnp.dot(ya.astype(BF16), wo_ref[:POOL_WIDTH, :], preferred_element_type=F32)
           + jnp.dot(yb.astype(BF16), wo_ref[POOL_WIDTH:, :], preferred_element_type=F32))
    o_ref[0] = x_ref[0] + mod_ref[0, 2:3, :] * out


def _mixer(x, p, q, k, v, mod_l, sinks, bias, pool_w_b, pool_scale, mix_g, w_out_b, ts):
    b, s, d = x.shape
    nb = ts // ATT_BLOCK
    nh = ts // POOL_HALO
    full = lambda shape: pl.BlockSpec(shape, lambda bi, i: (0,) * len(shape))
    row = lambda w: pl.BlockSpec((1, ts, w), lambda bi, i: (bi, i, 0))
    prev_blk = pl.BlockSpec((1, ATT_BLOCK, KV_WIDTH),
                            lambda bi, i: (bi, jnp.maximum(i * nb - 1, 0), 0))
    return pl.pallas_call(
        functools.partial(_mixer_kernel, ts=ts),
        grid=(b, s // ts),
        in_specs=[
            pl.BlockSpec(memory_space=pltpu.SMEM),
            row(d),
            row(POOL_WIDTH),
            pl.BlockSpec((1, POOL_HALO, POOL_WIDTH),
                         lambda bi, i: (bi, jnp.maximum(i * nh - 1, 0), 0)),
            row(ATTN_WIDTH),
            row(KV_WIDTH), prev_blk,
            row(KV_WIDTH), prev_blk,
            pl.BlockSpec((1, N_MOD, d), lambda bi, i: (bi, 0, 0)),
            full((N_HEADS, ATT_BLOCK, 2 * ATT_BLOCK)),
            full((len(POOL_WINDOWS), POOL_GROUP, POOL_GROUP)),
            full((1, POOL_WIDTH)),
            full((1, d)),
            full((d, d)),
        ],
        out_specs=row(d),
        out_shape=jax.ShapeDtypeStruct((b, s, d), F32),
        scratch_shapes=[
            pltpu.VMEM((ts + ATT_BLOCK, KV_WIDTH), BF16),
            pltpu.VMEM((ts + ATT_BLOCK, KV_WIDTH), BF16),
            pltpu.VMEM((ts, ATTN_WIDTH), F32),
            pltpu.VMEM((ts, POOL_WIDTH), F32),
        ],
        compiler_params=_cparams(("parallel", "parallel")),
        name="mixer",
    )(sinks, x, p, p, q, k, k, v, v, mod_l, bias, pool_w_b, pool_scale, mix_g, w_out_b)


def _oddeven_merge(lo, hi, r):
    step = r * 2
    if step < hi - lo:
        yield from _oddeven_merge(lo, hi, step)
        yield from _oddeven_merge(lo + r, hi, step)
        yield from [(i, i + r) for i in range(lo + r, hi - r, step)]
    else:
        yield (lo, lo + r)


def _oddeven_sort(lo, hi):
    if hi - lo >= 1:
        mid = lo + (hi - lo) // 2
        yield from _oddeven_sort(lo, mid)
        yield from _oddeven_sort(mid + 1, hi)
        yield from _oddeven_merge(lo, hi, 1)


_SORT16 = tuple(_oddeven_sort(0, PEER_TOPK - 1))
_BITONIC16 = tuple((i, i + st) for st in (8, 4, 2, 1) for i in range(PEER_TOPK) if not i & st)
SUBLANES = 8


def _compare_exchange(x, net):
    for i, j in net:
        if j < len(x):
            x[i], x[j] = jnp.maximum(x[i], x[j]), jnp.minimum(x[i], x[j])
    return x


def _top16_sorted(rows):
    x = _compare_exchange(list(rows), _SORT16)
    x = x + [jnp.full_like(x[0], NEG)] * (PEER_TOPK - len(x))
    for shift in (4, 2, 1):
        y = [pltpu.roll(v, shift, 0) for v in x]
        x = [jnp.maximum(x[v], y[PEER_TOPK - 1 - v]) for v in range(PEER_TOPK)]
        x = _compare_exchange(x, _BITONIC16)
    return x


def _pack_rows(slabs, sub):
    out = slabs[-1]
    for k in range(len(slabs) - 2, -1, -1):
        out = jnp.where(sub == k, slabs[k], out)
    return out


def _route_kernel(x_ref, mod_ref, g_ref, wq_ref, keys_ref,
                  h_ref, r2_ref, e2_ref, n1_ref, w1_ref, s_ref, *, tt):
    x = x_ref[...]
    ms = jnp.mean(x * x, axis=-1, keepdims=True)
    y = x * lax.rsqrt(ms + RMS_EPS) * g_ref[...]
    h = (y * (1.0 + mod_ref[0, 4:5, :]) + mod_ref[0, 3:4, :]).astype(BF16)
    h_ref[...] = h
    q = jnp.dot(h, wq_ref[...], preferred_element_type=F32).astype(BF16)
    for hp in range(2 * PEER_HEADS):
        qhp = q[:, hp * D_HALF:(hp + 1) * D_HALF]
        s_ref[hp] = lax.dot_general(keys_ref[hp % 2], qhp, (((1,), (1,)), ((), ())),
                                    preferred_element_type=F32)

    sub = lax.broadcasted_iota(jnp.int32, (SUBLANES, LANES), 0)
    slabs = lambda m: [m[v * SUBLANES:(v + 1) * SUBLANES] for v in range(N_KEYS // SUBLANES)]
    one = lambda hit: jnp.where(hit, 1.0, 0.0)

    def per_chunk(idx, carry):
        head = idx // (tt // LANES)
        c0 = pl.multiple_of((idx % (tt // LANES)) * LANES, LANES)
        cols = pl.ds(c0, LANES)
        s1 = s_ref[2 * head, :, cols]
        s2 = s_ref[2 * head + 1, :, cols]
        a = _top16_sorted(slabs(s1))
        bt = _top16_sorted(slabs(s2))
        a_lo, a_hi = _pack_rows(a[:SUBLANES], sub), _pack_rows(a[SUBLANES:], sub)
        b_hi = _pack_rows(bt[SUBLANES:], sub)
        cand = [a_lo + bt[0], a_hi + bt[0], a_lo + bt[1]]
        for r2 in range(2, SUBLANES):
            cand.append(jnp.where(sub < PEER_TOPK // (r2 + 1), a_lo + bt[r2], NEG))
        cand.append(b_hi + a[0])
        thr = _top16_sorted(cand)[PEER_TOPK - 1]
        top = a[0] + bt[0]
        hits = [c >= thr for c in cand]
        z = None
        for c, hit in zip(cand, hits):
            zc = jnp.where(hit, jnp.exp(c - top), 0.0)
            z = zc if z is None else z + zc
        zinv = 1.0 / jnp.sum(z, axis=0, keepdims=True)
        cnt_lo = one(hits[0])
        for hit in hits[2:-1]:
            cnt_lo = cnt_lo + one(hit)
        tail = jnp.sum(one(hits[-1]), axis=0, keepdims=True)
        cnt_lo = cnt_lo + jnp.where(sub == 0, tail, 0.0)
        cnt_hi = one(hits[1])
        cnt = [jnp.broadcast_to((cnt_lo if r < SUBLANES else cnt_hi)[r % SUBLANES:r % SUBLANES + 1],
                                (SUBLANES, LANES)) for r in range(PEER_TOPK)]
        a0 = a[0][0:1]
        b0 = bt[0][0:1]
        for vp in range(N_KEYS // (2 * SUBLANES)):
            rk, ex = [], []
            for v in (2 * vp, 2 * vp + 1):
                rows = pl.ds(v * SUBLANES, SUBLANES)
                x1 = s1[v * SUBLANES:(v + 1) * SUBLANES]
                x2 = s2[v * SUBLANES:(v + 1) * SUBLANES]
                n1 = jnp.zeros_like(x1)
                rank2 = jnp.full_like(x2, float(PEER_TOPK))
                for r in range(PEER_TOPK - 1, -1, -1):
                    n1 = jnp.where(x1 == a[r], cnt[r], n1)
                    rank2 = jnp.where(x2 >= bt[r], float(r), rank2)
                n1_ref[head, rows, cols] = n1
                w1_ref[head, rows, cols] = jnp.exp(x1 - a0) * zinv
                rk.append(rank2)
                ex.append(jnp.exp(x2 - b0))
            rows2 = pl.ds(vp * 2 * SUBLANES, 2 * SUBLANES)
            r2_ref[head, rows2, cols] = jnp.concatenate(rk, axis=0).astype(BF16)
            e2_ref[head, rows2, cols] = jnp.concatenate(ex, axis=0).astype(BF16)
        return carry

    lax.fori_loop(0, PEER_HEADS * (tt // LANES), per_chunk, 0)


def _route(x2, mod_l, norm_g, wq_b, keys_b, tt, s_len):
    t, d = x2.shape
    per_b = s_len // tt
    full = lambda shape: pl.BlockSpec(shape, lambda i: (0,) * len(shape))
    stat = lambda: pl.BlockSpec((PEER_HEADS, N_KEYS, tt), lambda i: (0, 0, i))
    return pl.pallas_call(
        functools.partial(_route_kernel, tt=tt),
        grid=(t // tt,),
        in_specs=[
            pl.BlockSpec((tt, d), lambda i: (i, 0)),
            pl.BlockSpec((1, N_MOD, d), lambda i: (i // per_b, 0, 0)),
            full((1, d)),
            full(wq_b.shape),
            full(keys_b.shape),
        ],
        out_specs=[pl.BlockSpec((tt, d), lambda i: (i, 0)), stat(), stat(), stat(), stat()],
        out_shape=[
            jax.ShapeDtypeStruct((t, d), BF16),
            jax.ShapeDtypeStruct((PEER_HEADS, N_KEYS, t), BF16),
            jax.ShapeDtypeStruct((PEER_HEADS, N_KEYS, t), BF16),
            jax.ShapeDtypeStruct((PEER_HEADS, N_KEYS, t), F32),
            jax.ShapeDtypeStruct((PEER_HEADS, N_KEYS, t), F32),
        ],
        scratch_shapes=[pltpu.VMEM((2 * PEER_HEADS, N_KEYS, tt), F32)],
        compiler_params=_cparams(("parallel",)),
        name="peer_route",
    )(x2, mod_l, norm_g, wq_b, keys_b)


def _peer_kernel(x_ref, h_ref, mod_ref, r2_ref, e2_ref, n1_ref, w1_ref, u_ref, vt_ref,
                 o_ref, acc_ref, act_ref, *, n_sub):
    e = pl.program_id(1)

    @pl.when(e == 0)
    def _():
        acc_ref[...] = jnp.zeros_like(acc_ref)

    a = lax.dot_general(u_ref[...], h_ref[...], (((1,), (1,)), ((), ())),
                        preferred_element_type=F32)
    for j in range(n_sub):
        gate = None
        for hd in range(PEER_HEADS):
            n1 = n1_ref[hd, j:j + 1, :].astype(BF16)
            w1 = w1_ref[hd, j:j + 1, :].astype(BF16)
            term = jnp.where(r2_ref[hd] < n1, e2_ref[hd] * w1, jnp.zeros((), BF16))
            gate = term if gate is None else gate + term
        aj = a[j * N_KEYS:(j + 1) * N_KEYS, :]
        gelu = 0.5 * aj * (1.0 + lax.erf(aj * (2.0 ** -0.5)))
        act_ref[j * N_KEYS:(j + 1) * N_KEYS, :] = (gelu * gate.astype(F32)).astype(BF16)
    acc_ref[...] += jnp.dot(vt_ref[...], act_ref[...], preferred_element_type=F32)

    @pl.when(e == pl.num_programs(1) - 1)
    def _():
        o_ref[...] = x_ref[...] + mod_ref[0, 5:6, :] * acc_ref[...].T


def _peer(x2, h2, mod_l, r2, e2, n1, w1, u_b, vt_b, layer, tt, e_blk, s_len):
    t, d = x2.shape
    n_sub = e_blk // N_KEYS
    per_b = s_len // tt
    stat = lambda: pl.BlockSpec((PEER_HEADS, N_KEYS, tt), lambda i, e: (0, 0, i))
    rows = lambda: pl.BlockSpec((PEER_HEADS, n_sub, tt), lambda i, e: (0, e, i))
    return pl.pallas_call(
        functools.partial(_peer_kernel, n_sub=n_sub),
        grid=(t // tt, N_EXPERTS // e_blk),
        in_specs=[
            pl.BlockSpec((tt, d), lambda i, e: (i, 0)),
            pl.BlockSpec((tt, d), lambda i, e: (i, 0)),
            pl.BlockSpec((1, N_MOD, d), lambda i, e: (i // per_b, 0, 0)),
            stat(), stat(), rows(), rows(),
            pl.BlockSpec((None, e_blk, d), lambda i, e: (layer, e, 0)),
            pl.BlockSpec((None, d, e_blk), lambda i, e: (layer, 0, e)),
        ],
        out_specs=pl.BlockSpec((tt, d), lambda i, e: (i, 0)),
        out_shape=jax.ShapeDtypeStruct((t, d), F32),
        scratch_shapes=[pltpu.VMEM((d, tt), F32), pltpu.VMEM((e_blk, tt), BF16)],
        compiler_params=_cparams(("parallel", "arbitrary")),
        name="peer_dense",
    )(x2, h2, mod_l, r2, e2, n1, w1, u_b, vt_b)


def _pair_heads(w, axis):
    shape = w.shape
    w = w.reshape(shape[:axis] + (2, N_HEADS // 2, HEAD_DIM) + shape[axis + 1:])
    return jnp.swapaxes(w, axis, axis + 1).reshape(shape)


def _attn_bias():
    qi = jnp.arange(ATT_BLOCK)[:, None]
    sj = jnp.arange(2 * ATT_BLOCK)[None, :]
    dist = qi + ATT_BLOCK - sj
    ok = (dist >= 0) & (dist < ATT_BLOCK)
    slopes = jnp.exp2(-8.0 * jnp.arange(1, N_HEADS + 1, dtype=F32) / N_HEADS)
    bias = -slopes[:, None, None] * dist.astype(F32)[None]
    return jnp.where(ok[None], bias, NEG)


def kernel(x, c, w_ada, b_ada, norm1_g, norm2_g, w_in, pool_w, pool_scale, q_norm_g, k_norm_g,
           attn_sinks, mix_norm_g, w_out, peer_wq, peer_subkeys, peer_u, peer_v):
    b, s, d = x.shape
    depth = w_ada.shape[0]
    t = b * s
    ts = min(512, s)
    tt_route = min(512, s)
    tt_peer = min(1024, s)
    e_blk = 1024

    q0, k0 = POOL_WIDTH, POOL_WIDTH + ATTN_WIDTH
    w_in_b = jnp.concatenate([w_in[:, :, :q0], _pair_heads(w_in[:, :, q0:k0], 2), w_in[:, :, k0:]],
                             axis=2).astype(BF16)
    w_out_b = jnp.concatenate([w_out[:, :q0], _pair_heads(w_out[:, q0:], 1)], axis=1).astype(BF16)
    mix_g = jnp.concatenate([mix_norm_g[:, :q0], _pair_heads(mix_norm_g[:, q0:], 1)], axis=1)
    pool_w_b = pool_w.astype(BF16)
    wq_b = peer_wq.astype(BF16)
    keys_b = peer_subkeys.astype(BF16)
    u_b = peer_u.astype(BF16)
    vt_b = jnp.swapaxes(peer_v, 1, 2).astype(BF16)
    qg_t = jnp.tile(q_norm_g, (1, N_HEADS))
    kg_t = jnp.tile(k_norm_g, (1, N_KV_HEADS))
    lane_head = jnp.arange(ATTN_WIDTH) // HEAD_DIM
    bd = jnp.where(lane_head[:, None] == lane_head[None, :], 1.0 / HEAD_DIM, 0.0).astype(BF16)
    bias = _attn_bias()

    mod = _adaln_mod(c, w_ada, b_ada).reshape(depth, b, N_MOD, d)

    for l in range(depth):
        p, q, k, v = _inproj(x, mod[l], norm1_g[l:l + 1], w_in_b[l], qg_t[l:l + 1], kg_t[l:l + 1], bd, ts)
        x = _mixer(x, p, q, k, v, mod[l], attn_sinks[l], bias, pool_w_b[l], pool_scale[l:l + 1],
                   mix_g[l:l + 1], w_out_b[l], ts)
        x2 = x.reshape(t, d)
        h2, r2, e2, n1, w1 = _route(x2, mod[l], norm2_g[l:l + 1], wq_b[l], keys_b[l], tt_route, s)
        x = _peer(x2, h2, mod[l], r2, e2, n1, w1, u_b, vt_b, l, tt_peer, e_blk, s).reshape(b, s, d)
    return x
```

```python
import functools

import jax
import jax.numpy as jnp
from jax import lax
from jax.experimental import pallas as pl
from jax.experimental.pallas import tpu as pltpu

F32 = jnp.float32
BF16 = jnp.bfloat16

D_MODEL = 1024
POOL_WIDTH = 512
POOL_WINDOWS = (2, 4, 8, 16)
POOL_GROUP = 128
POOL_HALO = 16
ATTN_WIDTH = 512
HEAD_DIM = 64
N_HEADS = 8
N_KV_HEADS = 2
KV_WIDTH = N_KV_HEADS * HEAD_DIM
ATT_BLOCK = 128
N_KEYS = 128
N_EXPERTS = N_KEYS * N_KEYS
PEER_HEADS = 8
PEER_TOPK = 16
D_HALF = 128
RMS_EPS = 1e-6
N_MOD = 6
NEG = -1e30

LANES = 128
SUBLANES = 8
BF16_ROWS = 2 * SUBLANES
VMEM_LIMIT = 56 * 1024 * 1024


def _cparams(sem):
    return pltpu.CompilerParams(dimension_semantics=sem, vmem_limit_bytes=VMEM_LIMIT)


def _mod_kernel(c_ref, w_ref, b_ref, o_ref):
    c = c_ref[...]
    cond = c * jax.nn.sigmoid(c)
    o_ref[0] = jnp.dot(cond, w_ref[0], preferred_element_type=F32,
                       precision=lax.Precision.HIGHEST) + b_ref[0]


def _adaln_mod(c, w_ada, b_ada, tn=1536):
    depth, d, n = w_ada.shape
    b = c.shape[0]
    return pl.pallas_call(
        _mod_kernel,
        grid=(depth, n // tn),
        in_specs=[
            pl.BlockSpec((b, d), lambda l, j: (0, 0)),
            pl.BlockSpec((1, d, tn), lambda l, j: (l, 0, j)),
            pl.BlockSpec((1, 1, tn), lambda l, j: (l, 0, j)),
        ],
        out_specs=pl.BlockSpec((1, b, tn), lambda l, j: (l, 0, j)),
        out_shape=jax.ShapeDtypeStruct((depth, b, n), F32),
        compiler_params=_cparams(("parallel", "parallel")),
        name="adaln_mod",
    )(c, w_ada, b_ada.reshape(depth, 1, n))


def _split_dot(t, bd):
    hi = t.astype(BF16)
    lo = (t - hi.astype(F32)).astype(BF16)
    return (jnp.dot(hi, bd, preferred_element_type=F32)
            + jnp.dot(lo, bd, preferred_element_type=F32))


def _inproj_kernel(x_ref, mod_ref, g_ref, w_ref, qg_ref, kg_ref, bd_ref,
                   p_ref, q_ref, k_ref, v_ref):
    x = x_ref[0]
    ms = jnp.mean(x * x, axis=-1, keepdims=True)
    y = x * lax.rsqrt(ms + RMS_EPS) * g_ref[...]
    h = y * (1.0 + mod_ref[0, 1:2, :]) + mod_ref[0, 0:1, :]
    proj = jnp.dot(h.astype(BF16), w_ref[...], preferred_element_type=F32)
    q0, k0, v0 = POOL_WIDTH, POOL_WIDTH + ATTN_WIDTH, POOL_WIDTH + ATTN_WIDTH + KV_WIDTH
    p_ref[0] = proj[:, :q0]
    q = proj[:, q0:k0]
    k = proj[:, k0:v0]
    bd = bd_ref[...]
    q_ms = _split_dot(q * q, bd)
    k_ms = _split_dot(k * k, bd[:KV_WIDTH, :KV_WIDTH])
    qn = q * lax.rsqrt(q_ms + RMS_EPS) * qg_ref[...]
    kn = k * lax.rsqrt(k_ms + RMS_EPS) * kg_ref[...]
    q_ref[0] = (qn * (HEAD_DIM ** -0.5)).astype(BF16)
    k_ref[0] = kn.astype(BF16)
    v_ref[0] = proj[:, v0:].astype(BF16)


def _inproj(x, mod_l, norm_g, w_in_b, qg_t, kg_t, bd, ts):
    b, s, d = x.shape
    n_in = w_in_b.shape[1]
    full = lambda shape: pl.BlockSpec(shape, lambda bi, i: (0,) * len(shape))
    return pl.pallas_call(
        _inproj_kernel,
        grid=(b, s // ts),
        in_specs=[
            pl.BlockSpec((1, ts, d), lambda bi, i: (bi, i, 0)),
            pl.BlockSpec((1, N_MOD, d), lambda bi, i: (bi, 0, 0)),
            full((1, d)),
            full((d, n_in)),
            full((1, ATTN_WIDTH)),
            full((1, KV_WIDTH)),
            full((ATTN_WIDTH, ATTN_WIDTH)),
        ],
        out_specs=[
            pl.BlockSpec((1, ts, POOL_WIDTH), lambda bi, i: (bi, i, 0)),
            pl.BlockSpec((1, ts, ATTN_WIDTH), lambda bi, i: (bi, i, 0)),
            pl.BlockSpec((1, ts, KV_WIDTH), lambda bi, i: (bi, i, 0)),
            pl.BlockSpec((1, ts, KV_WIDTH), lambda bi, i: (bi, i, 0)),
        ],
        out_shape=[
            jax.ShapeDtypeStruct((b, s, POOL_WIDTH), F32),
            jax.ShapeDtypeStruct((b, s, ATTN_WIDTH), BF16),
            jax.ShapeDtypeStruct((b, s, KV_WIDTH), BF16),
            jax.ShapeDtypeStruct((b, s, KV_WIDTH), BF16),
        ],
        compiler_params=_cparams(("parallel", "parallel")),
        name="inproj",
    )(x, mod_l, norm_g, w_in_b, qg_t, kg_t, bd)


def _mixer_kernel(sink_ref, x_ref, p_ref, pp_ref, q_ref, k_ref, kp_ref, v_ref, vp_ref,
                  mod_ref, bias_ref, pw_ref, ps_ref, mg_ref, wo_ref, o_ref,
                  kext_ref, vext_ref, attn_ref, pool_ref, *, ts):
    i = pl.program_id(1)
    nblk = ts // ATT_BLOCK

    has_prev = (i > 0).astype(F32)
    pext = jnp.concatenate([pp_ref[0] * has_prev, p_ref[0]], axis=0)
    pos1 = (lax.broadcasted_iota(jnp.int32, (ts, 1), 0) + i * ts + 1).astype(F32)
    for g, w in enumerate(POOL_WINDOWS):
        xg = pext[:, g * POOL_GROUP:(g + 1) * POOL_GROUP]
        sg = xg
        step = 1
        while step < w:
            sg = sg + pltpu.roll(sg, step, 0)
            step *= 2
        pooled = sg[POOL_HALO:] / jnp.minimum(pos1, float(w)) - xg[POOL_HALO:]
        mixed = jnp.dot(pooled.astype(BF16), pw_ref[g], preferred_element_type=F32)
        pool_ref[:, g * POOL_GROUP:(g + 1) * POOL_GROUP] = (
            mixed * ps_ref[:, g * POOL_GROUP:(g + 1) * POOL_GROUP])

    kext_ref[0:ATT_BLOCK] = kp_ref[0]
    kext_ref[ATT_BLOCK:] = k_ref[0]
    vext_ref[0:ATT_BLOCK] = vp_ref[0]
    vext_ref[ATT_BLOCK:] = v_ref[0]
    lane = lax.broadcasted_iota(jnp.int32, (ATT_BLOCK, LANES), 1)
    low_half = lane < HEAD_DIM
    col = lax.broadcasted_iota(jnp.int32, (1, 2 * ATT_BLOCK), 1)
    zero_b = jnp.zeros((ATT_BLOCK, LANES), BF16)
    for j in range(nblk):
        kb = kext_ref[j * ATT_BLOCK:(j + 2) * ATT_BLOCK]
        vb = vext_ref[j * ATT_BLOCK:(j + 2) * ATT_BLOCK]
        qs = q_ref[0, j * ATT_BLOCK:(j + 1) * ATT_BLOCK, :]
        stack = []
        for r in range(N_HEADS):
            slab = qs[:, (r % 4) * LANES:(r % 4 + 1) * LANES]
            keep = low_half if r < 4 else jnp.logical_not(low_half)
            stack.append(jnp.where(keep, slab, zero_b))
        qst = jnp.concatenate(stack, axis=0)
        scores = lax.dot_general(qst, kb, (((1,), (1,)), ((), ())),
                                 preferred_element_type=F32)
        first = jnp.logical_and(i == 0, j == 0)
        pen = jnp.where(jnp.logical_and(first, col < ATT_BLOCK), NEG, 0.0)
        probs = []
        inv = []
        for r in range(N_HEADS):
            sr = scores[r * ATT_BLOCK:(r + 1) * ATT_BLOCK] + bias_ref[r] + pen
            sink = sink_ref[r]
            m = jnp.maximum(jnp.max(sr, axis=-1, keepdims=True), sink)
            e = jnp.exp(sr - m)
            den = jnp.sum(e, axis=-1, keepdims=True) + jnp.exp(sink - m)
            probs.append(e.astype(BF16))
            inv.append(1.0 / den)
        pst = jnp.concatenate(probs, axis=0)
        pv = jnp.dot(pst, vb, preferred_element_type=F32)
        for m4 in range(4):
            lo = pv[m4 * ATT_BLOCK:(m4 + 1) * ATT_BLOCK] * inv[m4]
            hi = pv[(m4 + 4) * ATT_BLOCK:(m4 + 5) * ATT_BLOCK] * inv[m4 + 4]
            attn_ref[j * ATT_BLOCK:(j + 1) * ATT_BLOCK, m4 * LANES:(m4 + 1) * LANES] = (
                jnp.where(low_half, lo, hi))

    ya = pool_ref[...]
    yb = attn_ref[...]
    ya = ya * lax.rsqrt(jnp.mean(ya * ya, axis=-1, keepdims=True) + RMS_EPS) * mg_ref[:, :POOL_WIDTH]
    yb = yb * lax.rsqrt(jnp.mean(yb * yb, axis=-1, keepdims=True) + RMS_EPS) * mg_ref[:, POOL_WIDTH:]
    out = (jnp.dot(ya.astype(BF16), wo_ref[:POOL_WIDTH, :], preferred_element_type=F32)
           + jnp.dot(yb.astype(BF16), wo_ref[POOL_WIDTH:, :], preferred_element_type=F32))
    o_ref[0] = x_ref[0] + mod_ref[0, 2:3, :] * out


def _mixer(x, p, q, k, v, mod_l, sinks, bias, pool_w_b, pool_scale, mix_g, w_out_b, ts):
    b, s, d = x.shape
    nb = ts // ATT_BLOCK
    nh = ts // POOL_HALO
    full = lambda shape: pl.BlockSpec(shape, lambda bi, i: (0,) * len(shape))
    row = lambda w: pl.BlockSpec((1, ts, w), lambda bi, i: (bi, i, 0))
    prev_blk = pl.BlockSpec((1, ATT_BLOCK, KV_WIDTH),
                            lambda bi, i: (bi, jnp.maximum(i * nb - 1, 0), 0))
    return pl.pallas_call(
        functools.partial(_mixer_kernel, ts=ts),
        grid=(b, s // ts),
        in_specs=[
            pl.BlockSpec(memory_space=pltpu.SMEM),
            row(d),
            row(POOL_WIDTH),
            pl.BlockSpec((1, POOL_HALO, POOL_WIDTH),
                         lambda bi, i: (bi, jnp.maximum(i * nh - 1, 0), 0)),
            row(ATTN_WIDTH),
            row(KV_WIDTH), prev_blk,
            row(KV_WIDTH), prev_blk,
            pl.BlockSpec((1, N_MOD, d), lambda bi, i: (bi, 0, 0)),
            full((N_HEADS, ATT_BLOCK, 2 * ATT_BLOCK)),
            full((len(POOL_WINDOWS), POOL_GROUP, POOL_GROUP)),
            full((1, POOL_WIDTH)),
            full((1, d)),
            full((d, d)),
        ],
        out_specs=row(d),
        out_shape=jax.ShapeDtypeStruct((b, s, d), F32),
        scratch_shapes=[
            pltpu.VMEM((ts + ATT_BLOCK, KV_WIDTH), BF16),
            pltpu.VMEM((ts + ATT_BLOCK, KV_WIDTH), BF16),
            pltpu.VMEM((ts, ATTN_WIDTH), F32),
            pltpu.VMEM((ts, POOL_WIDTH), F32),
        ],
        compiler_params=_cparams(("parallel", "parallel")),
        name="mixer",
    )(sinks, x, p, p, q, k, k, v, v, mod_l, bias, pool_w_b, pool_scale, mix_g, w_out_b)


def _oddeven_merge(lo, hi, r):
    step = r * 2
    if step < hi - lo:
        yield from _oddeven_merge(lo, hi, step)
        yield from _oddeven_merge(lo + r, hi, step)
        yield from [(i, i + r) for i in range(lo + r, hi - r, step)]
    else:
        yield (lo, lo + r)


def _oddeven_sort(lo, hi):
    if hi - lo >= 1:
        mid = lo + (hi - lo) // 2
        yield from _oddeven_sort(lo, mid)
        yield from _oddeven_sort(mid + 1, hi)
        yield from _oddeven_merge(lo, hi, 1)


_SORT16 = tuple(_oddeven_sort(0, PEER_TOPK - 1))
_BITONIC16 = tuple((i, i + st) for st in (8, 4, 2, 1) for i in range(PEER_TOPK) if not i & st)


def _compare_exchange(x, net):
    for i, j in net:
        if j < len(x):
            x[i], x[j] = jnp.maximum(x[i], x[j]), jnp.minimum(x[i], x[j])
    return x


def _top16_sorted(rows):
    x = _compare_exchange(list(rows), _SORT16)
    x = x + [jnp.full_like(x[0], NEG)] * (PEER_TOPK - len(x))
    for shift in (4, 2, 1):
        y = [pltpu.roll(v, shift, 0) for v in x]
        x = [jnp.maximum(x[v], y[PEER_TOPK - 1 - v]) for v in range(PEER_TOPK)]
        x = _compare_exchange(x, _BITONIC16)
    return x


def _pack_rows(slabs, sub):
    out = slabs[-1]
    for k in range(len(slabs) - 2, -1, -1):
        out = jnp.where(sub == k, slabs[k], out)
    return out


def _route_kernel(x_ref, mod_ref, g_ref, wq_ref, keys_ref,
                  h_ref, r2_ref, e2_ref, n1_ref, w1_ref, s_ref, *, tt):
    x = x_ref[...]
    ms = jnp.mean(x * x, axis=-1, keepdims=True)
    y = x * lax.rsqrt(ms + RMS_EPS) * g_ref[...]
    h = (y * (1.0 + mod_ref[0, 4:5, :]) + mod_ref[0, 3:4, :]).astype(BF16)
    h_ref[...] = h
    q = jnp.dot(h, wq_ref[...], preferred_element_type=F32).astype(BF16)
    for hp in range(2 * PEER_HEADS):
        qhp = q[:, hp * D_HALF:(hp + 1) * D_HALF]
        s_ref[hp] = lax.dot_general(keys_ref[hp % 2], qhp, (((1,), (1,)), ((), ())),
                                    preferred_element_type=F32)

    sub = lax.broadcasted_iota(jnp.int32, (SUBLANES, LANES), 0)
    slabs = lambda m: [m[v * SUBLANES:(v + 1) * SUBLANES] for v in range(N_KEYS // SUBLANES)]
    one = lambda hit: jnp.where(hit, 1.0, 0.0)

    def per_chunk(idx, carry):
        head = idx // (tt // LANES)
        chunk = idx % (tt // LANES)
        cols = pl.ds(pl.multiple_of(chunk * LANES, LANES), LANES)
        s1 = s_ref[2 * head, :, cols]
        s2 = s_ref[2 * head + 1, :, cols]
        a = _top16_sorted(slabs(s1))
        bt = _top16_sorted(slabs(s2))
        a_lo, a_hi = _pack_rows(a[:SUBLANES], sub), _pack_rows(a[SUBLANES:], sub)
        b_hi = _pack_rows(bt[SUBLANES:], sub)
        cand = [a_lo + bt[0], a_hi + bt[0], a_lo + bt[1]]
        for r2 in range(2, SUBLANES):
            cand.append(jnp.where(sub < PEER_TOPK // (r2 + 1), a_lo + bt[r2], NEG))
        cand.append(b_hi + a[0])
        thr = _top16_sorted(cand)[PEER_TOPK - 1]
        top = a[0] + bt[0]
        hits = [c >= thr for c in cand]
        z = None
        for c, hit in zip(cand, hits):
            zc = jnp.where(hit, jnp.exp(c - top), 0.0)
            z = zc if z is None else z + zc
        zinv = 0.5 / jnp.sum(z, axis=0, keepdims=True)
        cnt_lo = one(hits[0])
        for hit in hits[2:-1]:
            cnt_lo = cnt_lo + one(hit)
        tail = jnp.sum(one(hits[-1]), axis=0, keepdims=True)
        cnt_lo = cnt_lo + jnp.where(sub == 0, tail, 0.0)
        cnt_hi = one(hits[1])
        cnt = [jnp.broadcast_to((cnt_lo if r < SUBLANES else cnt_hi)[r % SUBLANES:r % SUBLANES + 1],
                                (SUBLANES, LANES)) for r in range(PEER_TOPK)]
        a0 = a[0][0:1]
        b0 = bt[0][0:1]
        for vp in range(N_KEYS // (2 * SUBLANES)):
            rk, ex = [], []
            for v in (2 * vp, 2 * vp + 1):
                rows = pl.ds(v * SUBLANES, SUBLANES)
                x1 = s1[v * SUBLANES:(v + 1) * SUBLANES]
                x2 = s2[v * SUBLANES:(v + 1) * SUBLANES]
                n1 = jnp.zeros_like(x1)
                rank2 = jnp.full_like(x2, float(PEER_TOPK))
                for r in range(PEER_TOPK - 1, -1, -1):
                    n1 = jnp.where(x1 == a[r], cnt[r], n1)
                    rank2 = jnp.where(x2 >= bt[r], float(r), rank2)
                n1_ref[head, rows, cols] = n1
                w1_ref[head, rows, cols] = jnp.exp(x1 - a0) * zinv
                rk.append(rank2)
                ex.append(jnp.exp(x2 - b0))
            rows2 = pl.ds(vp * 2 * SUBLANES, 2 * SUBLANES)
            r2_ref[head, rows2, cols] = jnp.concatenate(rk, axis=0).astype(BF16)
            e2_ref[head, rows2, cols] = jnp.concatenate(ex, axis=0).astype(BF16)
        return carry

    lax.fori_loop(0, PEER_HEADS * (tt // LANES), per_chunk, 0)


def _route(x2, mod_l, norm_g, wq_b, keys_b, tt, s_len):
    t, d = x2.shape
    per_b = s_len // tt
    full = lambda shape: pl.BlockSpec(shape, lambda i: (0,) * len(shape))
    stat = lambda: pl.BlockSpec((PEER_HEADS, N_KEYS, tt), lambda i: (0, 0, i))
    return pl.pallas_call(
        functools.partial(_route_kernel, tt=tt),
        grid=(t // tt,),
        in_specs=[
            pl.BlockSpec((tt, d), lambda i: (i, 0)),
            pl.BlockSpec((1, N_MOD, d), lambda i: (i // per_b, 0, 0)),
            full((1, d)),
            full(wq_b.shape),
            full(keys_b.shape),
        ],
        out_specs=[pl.BlockSpec((tt, d), lambda i: (i, 0)), stat(), stat(), stat(), stat()],
        out_shape=[
            jax.ShapeDtypeStruct((t, d), BF16),
            jax.ShapeDtypeStruct((PEER_HEADS, N_KEYS, t), BF16),
            jax.ShapeDtypeStruct((PEER_HEADS, N_KEYS, t), BF16),
            jax.ShapeDtypeStruct((PEER_HEADS, N_KEYS, t), F32),
            jax.ShapeDtypeStruct((PEER_HEADS, N_KEYS, t), F32),
        ],
        scratch_shapes=[pltpu.VMEM((2 * PEER_HEADS, N_KEYS, tt), F32)],
        compiler_params=_cparams(("parallel",)),
        name="peer_route",
    )(x2, mod_l, norm_g, wq_b, keys_b)


def _peer_kernel(x_ref, h_ref, mod_ref, r2_ref, e2_ref, n1_ref, w1_ref, u_ref, vt_ref,
                 o_ref, acc_ref, act_ref, gate_ref, *, n_sub):
    e = pl.program_id(1)
    tt = h_ref.shape[0]

    @pl.when(e == 0)
    def _():
        acc_ref[...] = jnp.zeros_like(acc_ref)

    def row_tile(ref, hd, j):
        return jnp.broadcast_to(ref[hd, j:j + 1, :], (BF16_ROWS, tt)).astype(BF16)

    for j in range(n_sub):
        gate = None
        for hd in range(PEER_HEADS):
            n1 = row_tile(n1_ref, hd, j)
            w1 = row_tile(w1_ref, hd, j)
            term = jnp.where(r2_ref[hd] < n1[None], e2_ref[hd] * w1[None], jnp.zeros((), BF16))
            gate = term if gate is None else gate + term
        gate_ref[j * N_KEYS:(j + 1) * N_KEYS, :] = gate.reshape(N_KEYS, tt)
    a = lax.dot_general(u_ref[...], h_ref[...], (((1,), (1,)), ((), ())),
                        preferred_element_type=F32).astype(BF16)
    act_ref[...] = (a * gate_ref[...]) * (1.0 + lax.erf(a * (2.0 ** -0.5)))
    acc_ref[...] += jnp.dot(vt_ref[...], act_ref[...], preferred_element_type=F32)

    @pl.when(e == pl.num_programs(1) - 1)
    def _():
        o_ref[...] = x_ref[...] + mod_ref[0, 5:6, :] * acc_ref[...].T


def _peer(x2, h2, mod_l, r2, e2, n1, w1, u_b, vt_b, layer, tt, e_blk, s_len):
    t, d = x2.shape
    n_sub = e_blk // N_KEYS
    per_b = s_len // tt
    r2 = r2.reshape(PEER_HEADS, N_KEYS // BF16_ROWS, BF16_ROWS, t)
    e2 = e2.reshape(PEER_HEADS, N_KEYS // BF16_ROWS, BF16_ROWS, t)
    stat = lambda: pl.BlockSpec((PEER_HEADS, N_KEYS // BF16_ROWS, BF16_ROWS, tt), lambda i, e: (0, 0, 0, i))
    rows = lambda: pl.BlockSpec((PEER_HEADS, n_sub, tt), lambda i, e: (0, e, i))
    return pl.pallas_call(
        functools.partial(_peer_kernel, n_sub=n_sub),
        grid=(t // tt, N_EXPERTS // e_blk),
        in_specs=[
            pl.BlockSpec((tt, d), lambda i, e: (i, 0)),
            pl.BlockSpec((tt, d), lambda i, e: (i, 0)),
            pl.BlockSpec((1, N_MOD, d), lambda i, e: (i // per_b, 0, 0)),
            stat(), stat(), rows(), rows(),
            pl.BlockSpec((None, e_blk, d), lambda i, e: (layer, e, 0)),
            pl.BlockSpec((None, d, e_blk), lambda i, e: (layer, 0, e)),
        ],
        out_specs=pl.BlockSpec((tt, d), lambda i, e: (i, 0)),
        out_shape=jax.ShapeDtypeStruct((t, d), F32),
        scratch_shapes=[pltpu.VMEM((d, tt), F32), pltpu.VMEM((e_blk, tt), BF16),
                        pltpu.VMEM((e_blk, tt), BF16)],
        compiler_params=_cparams(("parallel", "arbitrary")),
        name="peer_dense",
    )(x2, h2, mod_l, r2, e2, n1, w1, u_b, vt_b)


def _pair_heads(w, axis):
    shape = w.shape
    w = w.reshape(shape[:axis] + (2, N_HEADS // 2, HEAD_DIM) + shape[axis + 1:])
    return jnp.swapaxes(w, axis, axis + 1).reshape(shape)


def _attn_bias():
    qi = jnp.arange(ATT_BLOCK)[:, None]
    sj = jnp.arange(2 * ATT_BLOCK)[None, :]
    dist = qi + ATT_BLOCK - sj
    ok = (dist >= 0) & (dist < ATT_BLOCK)
    slopes = jnp.exp2(-8.0 * jnp.arange(1, N_HEADS + 1, dtype=F32) / N_HEADS)
    bias = -slopes[:, None, None] * dist.astype(F32)[None]
    return jnp.where(ok[None], bias, NEG)


def kernel(x, c, w_ada, b_ada, norm1_g, norm2_g, w_in, pool_w, pool_scale, q_norm_g, k_norm_g,
           attn_sinks, mix_norm_g, w_out, peer_wq, peer_subkeys, peer_u, peer_v):
    b, s, d = x.shape
    depth = w_ada.shape[0]
    t = b * s
    ts = min(512, s)
    tt_route = min(512, s)
    tt_peer = min(1024, s)
    e_blk = 1024

    q0, k0 = POOL_WIDTH, POOL_WIDTH + ATTN_WIDTH
    w_in_b = jnp.concatenate([w_in[:, :, :q0], _pair_heads(w_in[:, :, q0:k0], 2), w_in[:, :, k0:]],
                             axis=2).astype(BF16)
    w_out_b = jnp.concatenate([w_out[:, :q0], _pair_heads(w_out[:, q0:], 1)], axis=1).astype(BF16)
    mix_g = jnp.concatenate([mix_norm_g[:, :q0], _pair_heads(mix_norm_g[:, q0:], 1)], axis=1)
    pool_w_b = pool_w.astype(BF16)
    wq_b = peer_wq.astype(BF16)
    keys_b = peer_subkeys.astype(BF16)
    u_b = peer_u.astype(BF16)
    vt_b = jnp.swapaxes(peer_v, 1, 2).astype(BF16)
    qg_t = jnp.tile(q_norm_g, (1, N_HEADS))
    kg_t = jnp.tile(k_norm_g, (1, N_KV_HEADS))
    lane_head = jnp.arange(ATTN_WIDTH) // HEAD_DIM
    bd = jnp.where(lane_head[:, None] == lane_head[None, :], 1.0 / HEAD_DIM, 0.0).astype(BF16)
    bias = _attn_bias()

    mod = _adaln_mod(c, w_ada, b_ada).reshape(depth, b, N_MOD, d)

    for l in range(depth):
        p, q, k, v = _inproj(x, mod[l], norm1_g[l:l + 1], w_in_b[l], qg_t[l:l + 1], kg_t[l:l + 1], bd, ts)
        x = _mixer(x, p, q, k, v, mod[l], attn_sinks[l], bias, pool_w_b[l], pool_scale[l:l + 1],
                   mix_g[l:l + 1], w_out_b[l], ts)
        x2 = x.reshape(t, d)
        h2, r2, e2, n1, w1 = _route(x2, mod[l], norm2_g[l:l + 1], wq_b[l], keys_b[l], tt_route, s)
        x = _peer(x2, h2, mod[l], r2, e2, n1, w1, u_b, vt_b, l, tt_peer, e_blk, s).reshape(b, s, d)
    return x
```

```python
import functools

import jax
import jax.numpy as jnp
from jax import lax
from jax.experimental import pallas as pl
from jax.experimental.pallas import tpu as pltpu

F32 = jnp.float32
BF16 = jnp.bfloat16

D_MODEL = 1024
POOL_WIDTH = 512
POOL_WINDOWS = (2, 4, 8, 16)
POOL_GROUP = 128
POOL_HALO = 16
ATTN_WIDTH = 512
HEAD_DIM = 64
N_HEADS = 8
N_KV_HEADS = 2
KV_WIDTH = N_KV_HEADS * HEAD_DIM
ATT_BLOCK = 128
N_KEYS = 128
N_EXPERTS = N_KEYS * N_KEYS
PEER_HEADS = 8
PEER_TOPK = 16
D_HALF = 128
RMS_EPS = 1e-6
N_MOD = 6
NEG = -1e30

LANES = 128
SUBLANES = 8
BF16_ROWS = 2 * SUBLANES
VMEM_LIMIT = 56 * 1024 * 1024


def _cparams(sem):
    return pltpu.CompilerParams(dimension_semantics=sem, vmem_limit_bytes=VMEM_LIMIT)


def _mod_kernel(c_ref, w_ref, b_ref, o_ref):
    c = c_ref[...]
    cond = c * jax.nn.sigmoid(c)
    o_ref[0] = jnp.dot(cond, w_ref[0], preferred_element_type=F32,
                       precision=lax.Precision.HIGHEST) + b_ref[0]


def _adaln_mod(c, w_ada, b_ada, tn=1536):
    depth, d, n = w_ada.shape
    b = c.shape[0]
    return pl.pallas_call(
        _mod_kernel,
        grid=(depth, n // tn),
        in_specs=[
            pl.BlockSpec((b, d), lambda l, j: (0, 0)),
            pl.BlockSpec((1, d, tn), lambda l, j: (l, 0, j)),
            pl.BlockSpec((1, 1, tn), lambda l, j: (l, 0, j)),
        ],
        out_specs=pl.BlockSpec((1, b, tn), lambda l, j: (l, 0, j)),
        out_shape=jax.ShapeDtypeStruct((depth, b, n), F32),
        compiler_params=_cparams(("parallel", "parallel")),
        name="adaln_mod",
    )(c, w_ada, b_ada.reshape(depth, 1, n))


def _split_dot(t, bd):
    hi = t.astype(BF16)
    lo = (t - hi.astype(F32)).astype(BF16)
    return (jnp.dot(hi, bd, preferred_element_type=F32)
            + jnp.dot(lo, bd, preferred_element_type=F32))


def _inproj_kernel(x_ref, mod_ref, g_ref, w_ref, qg_ref, kg_ref, bd_ref,
                   p_ref, q_ref, k_ref, v_ref):
    x = x_ref[0]
    ms = jnp.mean(x * x, axis=-1, keepdims=True)
    y = x * lax.rsqrt(ms + RMS_EPS) * g_ref[...]
    h = y * (1.0 + mod_ref[0, 1:2, :]) + mod_ref[0, 0:1, :]
    proj = jnp.dot(h.astype(BF16), w_ref[...], preferred_element_type=F32)
    q0, k0, v0 = POOL_WIDTH, POOL_WIDTH + ATTN_WIDTH, POOL_WIDTH + ATTN_WIDTH + KV_WIDTH
    p_ref[0] = proj[:, :q0]
    q = proj[:, q0:k0]
    k = proj[:, k0:v0]
    bd = bd_ref[...]
    q_ms = _split_dot(q * q, bd)
    k_ms = _split_dot(k * k, bd[:KV_WIDTH, :KV_WIDTH])
    qn = q * lax.rsqrt(q_ms + RMS_EPS) * qg_ref[...]
    kn = k * lax.rsqrt(k_ms + RMS_EPS) * kg_ref[...]
    q_ref[0] = (qn * (HEAD_DIM ** -0.5)).astype(BF16)
    k_ref[0] = kn.astype(BF16)
    v_ref[0] = proj[:, v0:].astype(BF16)


def _inproj(x, mod_l, norm_g, w_in_b, qg_t, kg_t, bd, ts):
    b, s, d = x.shape
    n_in = w_in_b.shape[1]
    full = lambda shape: pl.BlockSpec(shape, lambda bi, i: (0,) * len(shape))
    return pl.pallas_call(
        _inproj_kernel,
        grid=(b, s // ts),
        in_specs=[
            pl.BlockSpec((1, ts, d), lambda bi, i: (bi, i, 0)),
            pl.BlockSpec((1, N_MOD, d), lambda bi, i: (bi, 0, 0)),
            full((1, d)),
            full((d, n_in)),
            full((1, ATTN_WIDTH)),
            full((1, KV_WIDTH)),
            full((ATTN_WIDTH, ATTN_WIDTH)),
        ],
        out_specs=[
            pl.BlockSpec((1, ts, POOL_WIDTH), lambda bi, i: (bi, i, 0)),
            pl.BlockSpec((1, ts, ATTN_WIDTH), lambda bi, i: (bi, i, 0)),
            pl.BlockSpec((1, ts, KV_WIDTH), lambda bi, i: (bi, i, 0)),
            pl.BlockSpec((1, ts, KV_WIDTH), lambda bi, i: (bi, i, 0)),
        ],
        out_shape=[
            jax.ShapeDtypeStruct((b, s, POOL_WIDTH), F32),
            jax.ShapeDtypeStruct((b, s, ATTN_WIDTH), BF16),
            jax.ShapeDtypeStruct((b, s, KV_WIDTH), BF16),
            jax.ShapeDtypeStruct((b, s, KV_WIDTH), BF16),
        ],
        compiler_params=_cparams(("parallel", "parallel")),
        name="inproj",
    )(x, mod_l, norm_g, w_in_b, qg_t, kg_t, bd)


def _mixer_kernel(sink_ref, x_ref, p_ref, pp_ref, q_ref, k_ref, kp_ref, v_ref, vp_ref,
                  mod_ref, bias_ref, pw_ref, ps_ref, mg_ref, wo_ref, o_ref,
                  kext_ref, vext_ref, attn_ref, pool_ref, *, ts):
    i = pl.program_id(1)
    nblk = ts // ATT_BLOCK

    has_prev = (i > 0).astype(F32)
    pext = jnp.concatenate([pp_ref[0] * has_prev, p_ref[0]], axis=0)
    pos1 = (lax.broadcasted_iota(jnp.int32, (ts, 1), 0) + i * ts + 1).astype(F32)
    for g, w in enumerate(POOL_WINDOWS):
        xg = pext[:, g * POOL_GROUP:(g + 1) * POOL_GROUP]
        sg = xg
        step = 1
        while step < w:
            sg = sg + pltpu.roll(sg, step, 0)
            step *= 2
        pooled = sg[POOL_HALO:] / jnp.minimum(pos1, float(w)) - xg[POOL_HALO:]
        mixed = jnp.dot(pooled.astype(BF16), pw_ref[g], preferred_element_type=F32)
        pool_ref[:, g * POOL_GROUP:(g + 1) * POOL_GROUP] = (
            mixed * ps_ref[:, g * POOL_GROUP:(g + 1) * POOL_GROUP])

    kext_ref[0:ATT_BLOCK] = kp_ref[0]
    kext_ref[ATT_BLOCK:] = k_ref[0]
    vext_ref[0:ATT_BLOCK] = vp_ref[0]
    vext_ref[ATT_BLOCK:] = v_ref[0]
    lane = lax.broadcasted_iota(jnp.int32, (ATT_BLOCK, LANES), 1)
    low_half = lane < HEAD_DIM
    col = lax.broadcasted_iota(jnp.int32, (1, 2 * ATT_BLOCK), 1)
    zero_b = jnp.zeros((ATT_BLOCK, LANES), BF16)
    for j in range(nblk):
        kb = kext_ref[j * ATT_BLOCK:(j + 2) * ATT_BLOCK]
        vb = vext_ref[j * ATT_BLOCK:(j + 2) * ATT_BLOCK]
        qs = q_ref[0, j * ATT_BLOCK:(j + 1) * ATT_BLOCK, :]
        stack = []
        for r in range(N_HEADS):
            slab = qs[:, (r % 4) * LANES:(r % 4 + 1) * LANES]
            keep = low_half if r < 4 else jnp.logical_not(low_half)
            stack.append(jnp.where(keep, slab, zero_b))
        qst = jnp.concatenate(stack, axis=0)
        scores = lax.dot_general(qst, kb, (((1,), (1,)), ((), ())),
                                 preferred_element_type=F32)
        first = jnp.logical_and(i == 0, j == 0)
        pen = jnp.where(jnp.logical_and(first, col < ATT_BLOCK), NEG, 0.0)
        probs = []
        inv = []
        for r in range(N_HEADS):
            sr = scores[r * ATT_BLOCK:(r + 1) * ATT_BLOCK] + bias_ref[r] + pen
            sink = sink_ref[r]
            m = jnp.maximum(jnp.max(sr, axis=-1, keepdims=True), sink)
            e = jnp.exp(sr - m)
            den = jnp.sum(e, axis=-1, keepdims=True) + jnp.exp(sink - m)
            probs.append(e.astype(BF16))
            inv.append(1.0 / den)
        pst = jnp.concatenate(probs, axis=0)
        pv = jnp.dot(pst, vb, preferred_element_type=F32)
        for m4 in range(4):
            lo = pv[m4 * ATT_BLOCK:(m4 + 1) * ATT_BLOCK] * inv[m4]
            hi = pv[(m4 + 4) * ATT_BLOCK:(m4 + 5) * ATT_BLOCK] * inv[m4 + 4]
            attn_ref[j * ATT_BLOCK:(j + 1) * ATT_BLOCK, m4 * LANES:(m4 + 1) * LANES] = (
                jnp.where(low_half, lo, hi))

    ya = pool_ref[...]
    yb = attn_ref[...]
    ya = ya * lax.rsqrt(jnp.mean(ya * ya, axis=-1, keepdims=True) + RMS_EPS) * mg_ref[:, :POOL_WIDTH]
    yb = yb * lax.rsqrt(jnp.mean(yb * yb, axis=-1, keepdims=True) + RMS_EPS) * mg_ref[:, POOL_WIDTH:]
    out = (jnp.dot(ya.astype(BF16), wo_ref[:POOL_WIDTH, :], preferred_element_type=F32)
           + jnp.dot(yb.astype(BF16), wo_ref[POOL_WIDTH:, :], preferred_element_type=F32))
    o_ref[0] = x_ref[0] + mod_ref[0, 2:3, :] * out


def _mixer(x, p, q, k, v, mod_l, sinks, bias, pool_w_b, pool_scale, mix_g, w_out_b, ts):
    b, s, d = x.shape
    nb = ts // ATT_BLOCK
    nh = ts // POOL_HALO
    full = lambda shape: pl.BlockSpec(shape, lambda bi, i: (0,) * len(shape))
    row = lambda w: pl.BlockSpec((1, ts, w), lambda bi, i: (bi, i, 0))
    prev_blk = pl.BlockSpec((1, ATT_BLOCK, KV_WIDTH),
                            lambda bi, i: (bi, jnp.maximum(i * nb - 1, 0), 0))
    return pl.pallas_call(
        functools.partial(_mixer_kernel, ts=ts),
        grid=(b, s // ts),
        in_specs=[
            pl.BlockSpec(memory_space=pltpu.SMEM),
            row(d),
            row(POOL_WIDTH),
            pl.BlockSpec((1, POOL_HALO, POOL_WIDTH),
                         lambda bi, i: (bi, jnp.maximum(i * nh - 1, 0), 0)),
            row(ATTN_WIDTH),
            row(KV_WIDTH), prev_blk,
            row(KV_WIDTH), prev_blk,
            pl.BlockSpec((1, N_MOD, d), lambda bi, i: (bi, 0, 0)),
            full((N_HEADS, ATT_BLOCK, 2 * ATT_BLOCK)),
            full((len(POOL_WINDOWS), POOL_GROUP, POOL_GROUP)),
            full((1, POOL_WIDTH)),
            full((1, d)),
            full((d, d)),
        ],
        out_specs=row(d),
        out_shape=jax.ShapeDtypeStruct((b, s, d), F32),
        scratch_shapes=[
            pltpu.VMEM((ts + ATT_BLOCK, KV_WIDTH), BF16),
            pltpu.VMEM((ts + ATT_BLOCK, KV_WIDTH), BF16),
            pltpu.VMEM((ts, ATTN_WIDTH), F32),
            pltpu.VMEM((ts, POOL_WIDTH), F32),
        ],
        compiler_params=_cparams(("parallel", "parallel")),
        name="mixer",
    )(sinks, x, p, p, q, k, k, v, v, mod_l, bias, pool_w_b, pool_scale, mix_g, w_out_b)


def _oddeven_merge(lo, hi, r):
    step = r * 2
    if step < hi - lo:
        yield from _oddeven_merge(lo, hi, step)
        yield from _oddeven_merge(lo + r, hi, step)
        yield from [(i, i + r) for i in range(lo + r, hi - r, step)]
    else:
        yield (lo, lo + r)


def _oddeven_sort(lo, hi):
    if hi - lo >= 1:
        mid = lo + (hi - lo) // 2
        yield from _oddeven_sort(lo, mid)
        yield from _oddeven_sort(mid + 1, hi)
        yield from _oddeven_merge(lo, hi, 1)


_SORT16 = tuple(_oddeven_sort(0, PEER_TOPK - 1))
_BITONIC16 = tuple((i, i + st) for st in (8, 4, 2, 1) for i in range(PEER_TOPK) if not i & st)


def _compare_exchange(x, net):
    for i, j in net:
        if j < len(x):
            x[i], x[j] = jnp.maximum(x[i], x[j]), jnp.minimum(x[i], x[j])
    return x


def _top16_sorted(rows, smallest_only=False):
    x = _compare_exchange(list(rows), _SORT16)
    x = x + [jnp.full_like(x[0], NEG)] * (PEER_TOPK - len(x))
    for shift in (4, 2, 1):
        y = [pltpu.roll(v, shift, 0) for v in x]
        x = [jnp.maximum(x[v], y[PEER_TOPK - 1 - v]) for v in range(PEER_TOPK)]
        if smallest_only and shift == 1:
            return functools.reduce(jnp.minimum, x)
        x = _compare_exchange(x, _BITONIC16)
    return x


def _rank_among(x, tops):
    lt = [x < tops[7]]
    pick = lambda lo, hi, m: jnp.where(m, hi, lo)
    t4 = pick(tops[3], tops[11], lt[0])
    lt.append(x < t4)
    t2 = pick(pick(tops[1], tops[9], lt[0]), pick(tops[5], tops[13], lt[0]), lt[1])
    lt.append(x < t2)
    t1 = pick(pick(pick(tops[0], tops[8], lt[0]), pick(tops[4], tops[12], lt[0]), lt[1]),
              pick(pick(tops[2], tops[10], lt[0]), pick(tops[6], tops[14], lt[0]), lt[1]), lt[2])
    lt.append(x < t1)
    rank = functools.reduce(lambda r, mw: r + jnp.where(mw[0], mw[1], 0.0), zip(lt, (8.0, 4.0, 2.0, 1.0)), 0.0)
    return jnp.where(x < tops[PEER_TOPK - 1], float(PEER_TOPK), rank)


def _pack_rows(slabs, sub):
    out = slabs[-1]
    for k in range(len(slabs) - 2, -1, -1):
        out = jnp.where(sub == k, slabs[k], out)
    return out


def _route_kernel(x_ref, mod_ref, g_ref, wq_ref, keys_ref,
                  h_ref, r2_ref, e2_ref, n1_ref, w1_ref, s_ref, *, tt):
    x = x_ref[...]
    ms = jnp.mean(x * x, axis=-1, keepdims=True)
    y = x * lax.rsqrt(ms + RMS_EPS) * g_ref[...]
    h = (y * (1.0 + mod_ref[0, 4:5, :]) + mod_ref[0, 3:4, :]).astype(BF16)
    h_ref[...] = h
    q = jnp.dot(h, wq_ref[...], preferred_element_type=F32).astype(BF16)
    for hp in range(2 * PEER_HEADS):
        qhp = q[:, hp * D_HALF:(hp + 1) * D_HALF]
        s_ref[hp] = lax.dot_general(keys_ref[hp % 2], qhp, (((1,), (1,)), ((), ())),
                                    preferred_element_type=F32)

    sub = lax.broadcasted_iota(jnp.int32, (SUBLANES, LANES), 0)
    slabs = lambda m: [m[v * SUBLANES:(v + 1) * SUBLANES] for v in range(N_KEYS // SUBLANES)]
    one = lambda hit: jnp.where(hit, 1.0, 0.0)

    def per_chunk(idx, carry):
        head = idx // (tt // LANES)
        chunk = idx % (tt // LANES)
        cols = pl.ds(pl.multiple_of(chunk * LANES, LANES), LANES)
        s1 = s_ref[2 * head, :, cols]
        s2 = s_ref[2 * head + 1, :, cols]
        a = _top16_sorted(slabs(s1))
        bt = _top16_sorted(slabs(s2))
        a_lo, a_hi = _pack_rows(a[:SUBLANES], sub), _pack_rows(a[SUBLANES:], sub)
        b_hi = _pack_rows(bt[SUBLANES:], sub)
        cand = [a_lo + bt[0], a_hi + bt[0], a_lo + bt[1]]
        for r2 in range(2, SUBLANES):
            cand.append(jnp.where(sub < PEER_TOPK // (r2 + 1), a_lo + bt[r2], NEG))
        cand.append(b_hi + a[0])
        thr = _top16_sorted(cand, smallest_only=True)
        top = a[0] + bt[0]
        hits = [c >= thr for c in cand]
        z = None
        for c, hit in zip(cand, hits):
            zc = jnp.where(hit, jnp.exp(c - top), 0.0)
            z = zc if z is None else z + zc
        zinv = 0.5 / jnp.sum(z, axis=0, keepdims=True)
        cnt_lo = one(hits[0])
        for hit in hits[2:-1]:
            cnt_lo = cnt_lo + one(hit)
        tail = jnp.sum(one(hits[-1]), axis=0, keepdims=True)
        cnt_lo = cnt_lo + jnp.where(sub == 0, tail, 0.0)
        cnt_hi = one(hits[1])
        cnt = [jnp.broadcast_to((cnt_lo if r < SUBLANES else cnt_hi)[r % SUBLANES:r % SUBLANES + 1],
                                (SUBLANES, LANES)) for r in range(PEER_TOPK)]
        a0 = a[0][0:1]
        b0 = bt[0][0:1]
        for vp in range(N_KEYS // (2 * SUBLANES)):
            rk, ex = [], []
            for v in (2 * vp, 2 * vp + 1):
                rows = pl.ds(v * SUBLANES, SUBLANES)
                x1 = s1[v * SUBLANES:(v + 1) * SUBLANES]
                x2 = s2[v * SUBLANES:(v + 1) * SUBLANES]
                n1 = jnp.zeros_like(x1)
                for r in range(PEER_TOPK - 1, -1, -1):
                    n1 = jnp.where(x1 == a[r], cnt[r], n1)
                n1_ref[head, rows, cols] = n1
                w1_ref[head, rows, cols] = jnp.exp(x1 - a0) * zinv
                rk.append(_rank_among(x2, bt))
                ex.append(jnp.exp(x2 - b0))
            rows2 = pl.ds(vp * 2 * SUBLANES, 2 * SUBLANES)
            r2_ref[head, rows2, cols] = jnp.concatenate(rk, axis=0).astype(BF16)
            e2_ref[head, rows2, cols] = jnp.concatenate(ex, axis=0).astype(BF16)
        return carry

    lax.fori_loop(0, PEER_HEADS * (tt // LANES), per_chunk, 0)


def _route(x2, mod_l, norm_g, wq_b, keys_b, tt, s_len):
    t, d = x2.shape
    per_b = s_len // tt
    full = lambda shape: pl.BlockSpec(shape, lambda i: (0,) * len(shape))
    stat = lambda: pl.BlockSpec((PEER_HEADS, N_KEYS, tt), lambda i: (0, 0, i))
    return pl.pallas_call(
        functools.partial(_route_kernel, tt=tt),
        grid=(t // tt,),
        in_specs=[
            pl.BlockSpec((tt, d), lambda i: (i, 0)),
            pl.BlockSpec((1, N_MOD, d), lambda i: (i // per_b, 0, 0)),
            full((1, d)),
            full(wq_b.shape),
            full(keys_b.shape),
        ],
        out_specs=[pl.BlockSpec((tt, d), lambda i: (i, 0)), stat(), stat(), stat(), stat()],
        out_shape=[
            jax.ShapeDtypeStruct((t, d), BF16),
            jax.ShapeDtypeStruct((PEER_HEADS, N_KEYS, t), BF16),
            jax.ShapeDtypeStruct((PEER_HEADS, N_KEYS, t), BF16),
            jax.ShapeDtypeStruct((PEER_HEADS, N_KEYS, t), F32),
            jax.ShapeDtypeStruct((PEER_HEADS, N_KEYS, t), F32),
        ],
        scratch_shapes=[pltpu.VMEM((2 * PEER_HEADS, N_KEYS, tt), F32)],
        compiler_params=_cparams(("parallel",)),
        name="peer_route",
    )(x2, mod_l, norm_g, wq_b, keys_b)


def _peer_kernel(x_ref, h_ref, mod_ref, r2_ref, e2_ref, n1_ref, w1_ref, u_ref, vt_ref,
                 o_ref, acc_ref, act_ref, gate_ref, *, n_sub):
    e = pl.program_id(1)
    tt = h_ref.shape[0]

    @pl.when(e == 0)
    def _():
        acc_ref[...] = jnp.zeros_like(acc_ref)

    def row_tile(ref, hd, j):
        return jnp.broadcast_to(ref[hd, j:j + 1, :], (BF16_ROWS, tt)).astype(BF16)

    for j in range(n_sub):
        gate = None
        for hd in range(PEER_HEADS):
            n1 = row_tile(n1_ref, hd, j)
            w1 = row_tile(w1_ref, hd, j)
            term = jnp.where(r2_ref[hd] < n1[None], e2_ref[hd] * w1[None], jnp.zeros((), BF16))
            gate = term if gate is None else gate + term
        gate_ref[j * N_KEYS:(j + 1) * N_KEYS, :] = gate.reshape(N_KEYS, tt)
    a = lax.dot_general(u_ref[...], h_ref[...], (((1,), (1,)), ((), ())),
                        preferred_element_type=F32).astype(BF16)
    act_ref[...] = (a * gate_ref[...]) * (1.0 + lax.erf(a * (2.0 ** -0.5)))
    acc_ref[...] += jnp.dot(vt_ref[...], act_ref[...], preferred_element_type=F32)

    @pl.when(e == pl.num_programs(1) - 1)
    def _():
        o_ref[...] = x_ref[...] + mod_ref[0, 5:6, :] * acc_ref[...].T


def _peer(x2, h2, mod_l, r2, e2, n1, w1, u_b, vt_b, layer, tt, e_blk, s_len):
    t, d = x2.shape
    n_sub = e_blk // N_KEYS
    per_b = s_len // tt
    r2 = r2.reshape(PEER_HEADS, N_KEYS // BF16_ROWS, BF16_ROWS, t)
    e2 = e2.reshape(PEER_HEADS, N_KEYS // BF16_ROWS, BF16_ROWS, t)
    stat = lambda: pl.BlockSpec((PEER_HEADS, N_KEYS // BF16_ROWS, BF16_ROWS, tt), lambda i, e: (0, 0, 0, i))
    rows = lambda: pl.BlockSpec((PEER_HEADS, n_sub, tt), lambda i, e: (0, e, i))
    return pl.pallas_call(
        functools.partial(_peer_kernel, n_sub=n_sub),
        grid=(t // tt, N_EXPERTS // e_blk),
        in_specs=[
            pl.BlockSpec((tt, d), lambda i, e: (i, 0)),
            pl.BlockSpec((tt, d), lambda i, e: (i, 0)),
            pl.BlockSpec((1, N_MOD, d), lambda i, e: (i // per_b, 0, 0)),
            stat(), stat(), rows(), rows(),
            pl.BlockSpec((None, e_blk, d), lambda i, e: (layer, e, 0)),
            pl.BlockSpec((None, d, e_blk), lambda i, e: (layer, 0, e)),
        ],
        out_specs=pl.BlockSpec((tt, d), lambda i, e: (i, 0)),
        out_shape=jax.ShapeDtypeStruct((t, d), F32),
        scratch_shapes=[pltpu.VMEM((d, tt), F32), pltpu.VMEM((e_blk, tt), BF16),
                        pltpu.VMEM((e_blk, tt), BF16)],
        compiler_params=_cparams(("parallel", "arbitrary")),
        name="peer_dense",
    )(x2, h2, mod_l, r2, e2, n1, w1, u_b, vt_b)


def _pair_heads(w, axis):
    shape = w.shape
    w = w.reshape(shape[:axis] + (2, N_HEADS // 2, HEAD_DIM) + shape[axis + 1:])
    return jnp.swapaxes(w, axis, axis + 1).reshape(shape)


def _attn_bias():
    qi = jnp.arange(ATT_BLOCK)[:, None]
    sj = jnp.arange(2 * ATT_BLOCK)[None, :]
    dist = qi + ATT_BLOCK - sj
    ok = (dist >= 0) & (dist < ATT_BLOCK)
    slopes = jnp.exp2(-8.0 * jnp.arange(1, N_HEADS + 1, dtype=F32) / N_HEADS)
    bias = -slopes[:, None, None] * dist.astype(F32)[None]
    return jnp.where(ok[None], bias, NEG)


def kernel(x, c, w_ada, b_ada, norm1_g, norm2_g, w_in, pool_w, pool_scale, q_norm_g, k_norm_g,
           attn_sinks, mix_norm_g, w_out, peer_wq, peer_subkeys, peer_u, peer_v):
    b, s, d = x.shape
    depth = w_ada.shape[0]
    t = b * s
    ts = min(512, s)
    tt_route = min(512, s)
    tt_peer = min(512, s)
    e_blk = 2048

    q0, k0 = POOL_WIDTH, POOL_WIDTH + ATTN_WIDTH
    w_in_b = jnp.concatenate([w_in[:, :, :q0], _pair_heads(w_in[:, :, q0:k0], 2), w_in[:, :, k0:]],
                             axis=2).astype(BF16)
    w_out_b = jnp.concatenate([w_out[:, :q0], _pair_heads(w_out[:, q0:], 1)], axis=1).astype(BF16)
    mix_g = jnp.concatenate([mix_norm_g[:, :q0], _pair_heads(mix_norm_g[:, q0:], 1)], axis=1)
    pool_w_b = pool_w.astype(BF16)
    wq_b = peer_wq.astype(BF16)
    keys_b = peer_subkeys.astype(BF16)
    u_b = peer_u.astype(BF16)
    vt_b = jnp.swapaxes(peer_v, 1, 2).astype(BF16)
    qg_t = jnp.tile(q_norm_g, (1, N_HEADS))
    kg_t = jnp.tile(k_norm_g, (1, N_KV_HEADS))
    lane_head = jnp.arange(ATTN_WIDTH) // HEAD_DIM
    bd = jnp.where(lane_head[:, None] == lane_head[None, :], 1.0 / HEAD_DIM, 0.0).astype(BF16)
    bias = _attn_bias()

    mod = _adaln_mod(c, w_ada, b_ada).reshape(depth, b, N_MOD, d)

    for l in range(depth):
        p, q, k, v = _inproj(x, mod[l], norm1_g[l:l + 1], w_in_b[l], qg_t[l:l + 1], kg_t[l:l + 1], bd, ts)
        x = _mixer(x, p, q, k, v, mod[l], attn_sinks[l], bias, pool_w_b[l], pool_scale[l:l + 1],
                   mix_g[l:l + 1], w_out_b[l], ts)
        x2 = x.reshape(t, d)
        h2, r2, e2, n1, w1 = _route(x2, mod[l], norm2_g[l:l + 1], wq_b[l], keys_b[l], tt_route, s)
        x = _peer(x2, h2, mod[l], r2, e2, n1, w1, u_b, vt_b, l, tt_peer, e_blk, s).reshape(b, s, d)
    return x
```

```python
import functools

import jax
import jax.numpy as jnp
from jax import lax
from jax.experimental import pallas as pl
from jax.experimental.pallas import tpu as pltpu

F32 = jnp.float32
BF16 = jnp.bfloat16

D_MODEL = 1024
POOL_WIDTH = 512
POOL_WINDOWS = (2, 4, 8, 16)
POOL_GROUP = 128
POOL_HALO = 16
ATTN_WIDTH = 512
HEAD_DIM = 64
N_HEADS = 8
N_KV_HEADS = 2
KV_WIDTH = N_KV_HEADS * HEAD_DIM
ATT_BLOCK = 128
N_KEYS = 128
N_EXPERTS = N_KEYS * N_KEYS
PEER_HEADS = 8
PEER_TOPK = 16
D_HALF = 128
RMS_EPS = 1e-6
N_MOD = 6
NEG = -1e30

LANES = 128
SUBLANES = 8
BF16_ROWS = 2 * SUBLANES
VMEM_LIMIT = 56 * 1024 * 1024


def _cparams(sem):
    return pltpu.CompilerParams(dimension_semantics=sem, vmem_limit_bytes=VMEM_LIMIT)


def _mod_kernel(c_ref, w_ref, b_ref, o_ref):
    c = c_ref[...]
    cond = c * jax.nn.sigmoid(c)
    o_ref[0] = jnp.dot(cond, w_ref[0], preferred_element_type=F32,
                       precision=lax.Precision.HIGHEST) + b_ref[0]


def _adaln_mod(c, w_ada, b_ada, tn=1536):
    depth, d, n = w_ada.shape
    b = c.shape[0]
    return pl.pallas_call(
        _mod_kernel,
        grid=(depth, n // tn),
        in_specs=[
            pl.BlockSpec((b, d), lambda l, j: (0, 0)),
            pl.BlockSpec((1, d, tn), lambda l, j: (l, 0, j)),
            pl.BlockSpec((1, 1, tn), lambda l, j: (l, 0, j)),
        ],
        out_specs=pl.BlockSpec((1, b, tn), lambda l, j: (l, 0, j)),
        out_shape=jax.ShapeDtypeStruct((depth, b, n), F32),
        compiler_params=_cparams(("parallel", "parallel")),
        name="adaln_mod",
    )(c, w_ada, b_ada.reshape(depth, 1, n))


def _split_dot(t, bd):
    hi = t.astype(BF16)
    lo = (t - hi.astype(F32)).astype(BF16)
    return (jnp.dot(hi, bd, preferred_element_type=F32)
            + jnp.dot(lo, bd, preferred_element_type=F32))


def _inproj_kernel(x_ref, mod_ref, g_ref, w_ref, qg_ref, kg_ref, bd_ref,
                   p_ref, q_ref, k_ref, v_ref):
    x = x_ref[0]
    ms = jnp.mean(x * x, axis=-1, keepdims=True)
    y = x * lax.rsqrt(ms + RMS_EPS) * g_ref[...]
    h = y * (1.0 + mod_ref[0, 1:2, :]) + mod_ref[0, 0:1, :]
    proj = jnp.dot(h.astype(BF16), w_ref[...], preferred_element_type=F32)
    q0, k0, v0 = POOL_WIDTH, POOL_WIDTH + ATTN_WIDTH, POOL_WIDTH + ATTN_WIDTH + KV_WIDTH
    p_ref[0] = proj[:, :q0]
    q = proj[:, q0:k0]
    k = proj[:, k0:v0]
    bd = bd_ref[...]
    q_ms = _split_dot(q * q, bd)
    k_ms = _split_dot(k * k, bd[:KV_WIDTH, :KV_WIDTH])
    qn = q * lax.rsqrt(q_ms + RMS_EPS) * qg_ref[...]
    kn = k * lax.rsqrt(k_ms + RMS_EPS) * kg_ref[...]
    q_ref[0] = (qn * (HEAD_DIM ** -0.5)).astype(BF16)
    k_ref[0] = kn.astype(BF16)
    v_ref[0] = proj[:, v0:].astype(BF16)


def _inproj(x, mod_l, norm_g, w_in_b, qg_t, kg_t, bd, ts):
    b, s, d = x.shape
    n_in = w_in_b.shape[1]
    full = lambda shape: pl.BlockSpec(shape, lambda bi, i: (0,) * len(shape))
    return pl.pallas_call(
        _inproj_kernel,
        grid=(b, s // ts),
        in_specs=[
            pl.BlockSpec((1, ts, d), lambda bi, i: (bi, i, 0)),
            pl.BlockSpec((1, N_MOD, d), lambda bi, i: (bi, 0, 0)),
            full((1, d)),
            full((d, n_in)),
            full((1, ATTN_WIDTH)),
            full((1, KV_WIDTH)),
            full((ATTN_WIDTH, ATTN_WIDTH)),
        ],
        out_specs=[
            pl.BlockSpec((1, ts, POOL_WIDTH), lambda bi, i: (bi, i, 0)),
            pl.BlockSpec((1, ts, ATTN_WIDTH), lambda bi, i: (bi, i, 0)),
            pl.BlockSpec((1, ts, KV_WIDTH), lambda bi, i: (bi, i, 0)),
            pl.BlockSpec((1, ts, KV_WIDTH), lambda bi, i: (bi, i, 0)),
        ],
        out_shape=[
            jax.ShapeDtypeStruct((b, s, POOL_WIDTH), F32),
            jax.ShapeDtypeStruct((b, s, ATTN_WIDTH), BF16),
            jax.ShapeDtypeStruct((b, s, KV_WIDTH), BF16),
            jax.ShapeDtypeStruct((b, s, KV_WIDTH), BF16),
        ],
        compiler_params=_cparams(("parallel", "parallel")),
        name="inproj",
    )(x, mod_l, norm_g, w_in_b, qg_t, kg_t, bd)


def _mixer_kernel(sink_ref, x_ref, p_ref, pp_ref, q_ref, k_ref, kp_ref, v_ref, vp_ref,
                  mod_ref, bias_ref, pw_ref, ps_ref, mg_ref, wo_ref, o_ref,
                  kext_ref, vext_ref, attn_ref, pool_ref, *, ts):
    i = pl.program_id(1)
    nblk = ts // ATT_BLOCK

    has_prev = (i > 0).astype(F32)
    pext = jnp.concatenate([pp_ref[0] * has_prev, p_ref[0]], axis=0)
    pos1 = (lax.broadcasted_iota(jnp.int32, (ts, 1), 0) + i * ts + 1).astype(F32)
    for g, w in enumerate(POOL_WINDOWS):
        xg = pext[:, g * POOL_GROUP:(g + 1) * POOL_GROUP]
        sg = xg
        step = 1
        while step < w:
            sg = sg + pltpu.roll(sg, step, 0)
            step *= 2
        pooled = sg[POOL_HALO:] / jnp.minimum(pos1, float(w)) - xg[POOL_HALO:]
        mixed = jnp.dot(pooled.astype(BF16), pw_ref[g], preferred_element_type=F32)
        pool_ref[:, g * POOL_GROUP:(g + 1) * POOL_GROUP] = (
            mixed * ps_ref[:, g * POOL_GROUP:(g + 1) * POOL_GROUP])

    kext_ref[0:ATT_BLOCK] = kp_ref[0]
    kext_ref[ATT_BLOCK:] = k_ref[0]
    vext_ref[0:ATT_BLOCK] = vp_ref[0]
    vext_ref[ATT_BLOCK:] = v_ref[0]
    lane = lax.broadcasted_iota(jnp.int32, (ATT_BLOCK, LANES), 1)
    low_half = lane < HEAD_DIM
    col = lax.broadcasted_iota(jnp.int32, (1, 2 * ATT_BLOCK), 1)
    zero_b = jnp.zeros((ATT_BLOCK, LANES), BF16)
    for j in range(nblk):
        kb = kext_ref[j * ATT_BLOCK:(j + 2) * ATT_BLOCK]
        vb = vext_ref[j * ATT_BLOCK:(j + 2) * ATT_BLOCK]
        qs = q_ref[0, j * ATT_BLOCK:(j + 1) * ATT_BLOCK, :]
        stack = []
        for r in range(N_HEADS):
            slab = qs[:, (r % 4) * LANES:(r % 4 + 1) * LANES]
            keep = low_half if r < 4 else jnp.logical_not(low_half)
            stack.append(jnp.where(keep, slab, zero_b))
        qst = jnp.concatenate(stack, axis=0)
        scores = lax.dot_general(qst, kb, (((1,), (1,)), ((), ())),
                                 preferred_element_type=F32)
        first = jnp.logical_and(i == 0, j == 0)
        pen = jnp.where(jnp.logical_and(first, col < ATT_BLOCK), NEG, 0.0)
        probs = []
        inv = []
        for r in range(N_HEADS):
            sr = scores[r * ATT_BLOCK:(r + 1) * ATT_BLOCK] + bias_ref[r] + pen
            sink = sink_ref[r]
            m = jnp.maximum(jnp.max(sr, axis=-1, keepdims=True), sink)
            e = jnp.exp(sr - m)
            den = jnp.sum(e, axis=-1, keepdims=True) + jnp.exp(sink - m)
            probs.append(e.astype(BF16))
            inv.append(1.0 / den)
        pst = jnp.concatenate(probs, axis=0)
        pv = jnp.dot(pst, vb, preferred_element_type=F32)
        for m4 in range(4):
            lo = pv[m4 * ATT_BLOCK:(m4 + 1) * ATT_BLOCK] * inv[m4]
            hi = pv[(m4 + 4) * ATT_BLOCK:(m4 + 5) * ATT_BLOCK] * inv[m4 + 4]
            attn_ref[j * ATT_BLOCK:(j + 1) * ATT_BLOCK, m4 * LANES:(m4 + 1) * LANES] = (
                jnp.where(low_half, lo, hi))

    ya = pool_ref[...]
    yb = attn_ref[...]
    ya = ya * lax.rsqrt(jnp.mean(ya * ya, axis=-1, keepdims=True) + RMS_EPS) * mg_ref[:, :POOL_WIDTH]
    yb = yb * lax.rsqrt(jnp.mean(yb * yb, axis=-1, keepdims=True) + RMS_EPS) * mg_ref[:, POOL_WIDTH:]
    out = (jnp.dot(ya.astype(BF16), wo_ref[:POOL_WIDTH, :], preferred_element_type=F32)
           + jnp.dot(yb.astype(BF16), wo_ref[POOL_WIDTH:, :], preferred_element_type=F32))
    o_ref[0] = x_ref[0] + mod_ref[0, 2:3, :] * out


def _mixer(x, p, q, k, v, mod_l, sinks, bias, pool_w_b, pool_scale, mix_g, w_out_b, ts):
    b, s, d = x.shape
    nb = ts // ATT_BLOCK
    nh = ts // POOL_HALO
    full = lambda shape: pl.BlockSpec(shape, lambda bi, i: (0,) * len(shape))
    row = lambda w: pl.BlockSpec((1, ts, w), lambda bi, i: (bi, i, 0))
    prev_blk = pl.BlockSpec((1, ATT_BLOCK, KV_WIDTH),
                            lambda bi, i: (bi, jnp.maximum(i * nb - 1, 0), 0))
    return pl.pallas_call(
        functools.partial(_mixer_kernel, ts=ts),
        grid=(b, s // ts),
        in_specs=[
            pl.BlockSpec(memory_space=pltpu.SMEM),
            row(d),
            row(POOL_WIDTH),
            pl.BlockSpec((1, POOL_HALO, POOL_WIDTH),
                         lambda bi, i: (bi, jnp.maximum(i * nh - 1, 0), 0)),
            row(ATTN_WIDTH),
            row(KV_WIDTH), prev_blk,
            row(KV_WIDTH), prev_blk,
            pl.BlockSpec((1, N_MOD, d), lambda bi, i: (bi, 0, 0)),
            full((N_HEADS, ATT_BLOCK, 2 * ATT_BLOCK)),
            full((len(POOL_WINDOWS), POOL_GROUP, POOL_GROUP)),
            full((1, POOL_WIDTH)),
            full((1, d)),
            full((d, d)),
        ],
        out_specs=row(d),
        out_shape=jax.ShapeDtypeStruct((b, s, d), F32),
        scratch_shapes=[
            pltpu.VMEM((ts + ATT_BLOCK, KV_WIDTH), BF16),
            pltpu.VMEM((ts + ATT_BLOCK, KV_WIDTH), BF16),
            pltpu.VMEM((ts, ATTN_WIDTH), F32),
            pltpu.VMEM((ts, POOL_WIDTH), F32),
        ],
        compiler_params=_cparams(("parallel", "parallel")),
        name="mixer",
    )(sinks, x, p, p, q, k, k, v, v, mod_l, bias, pool_w_b, pool_scale, mix_g, w_out_b)


def _oddeven_merge(lo, hi, r):
    step = r * 2
    if step < hi - lo:
        yield from _oddeven_merge(lo, hi, step)
        yield from _oddeven_merge(lo + r, hi, step)
        yield from [(i, i + r) for i in range(lo + r, hi - r, step)]
    else:
        yield (lo, lo + r)


def _oddeven_sort(lo, hi):
    if hi - lo >= 1:
        mid = lo + (hi - lo) // 2
        yield from _oddeven_sort(lo, mid)
        yield from _oddeven_sort(mid + 1, hi)
        yield from _oddeven_merge(lo, hi, 1)


_SORT16 = tuple(_oddeven_sort(0, PEER_TOPK - 1))
_BITONIC16 = tuple((i, i + st) for st in (8, 4, 2, 1) for i in range(PEER_TOPK) if not i & st)


def _compare_exchange(x, net):
    for i, j in net:
        if j < len(x):
            x[i], x[j] = jnp.maximum(x[i], x[j]), jnp.minimum(x[i], x[j])
    return x


def _top16_sorted(rows, smallest_only=False):
    x = _compare_exchange(list(rows), _SORT16)
    x = x + [jnp.full_like(x[0], NEG)] * (PEER_TOPK - len(x))
    for shift in (4, 2, 1):
        y = [pltpu.roll(v, shift, 0) for v in x]
        x = [jnp.maximum(x[v], y[PEER_TOPK - 1 - v]) for v in range(PEER_TOPK)]
        if smallest_only and shift == 1:
            return functools.reduce(jnp.minimum, x)
        x = _compare_exchange(x, _BITONIC16)
    return x


def _rank_among(x, tops):
    lt = [x < tops[7]]
    pick = lambda lo, hi, m: jnp.where(m, hi, lo)
    t4 = pick(tops[3], tops[11], lt[0])
    lt.append(x < t4)
    t2 = pick(pick(tops[1], tops[9], lt[0]), pick(tops[5], tops[13], lt[0]), lt[1])
    lt.append(x < t2)
    t1 = pick(pick(pick(tops[0], tops[8], lt[0]), pick(tops[4], tops[12], lt[0]), lt[1]),
              pick(pick(tops[2], tops[10], lt[0]), pick(tops[6], tops[14], lt[0]), lt[1]), lt[2])
    lt.append(x < t1)
    rank = functools.reduce(lambda r, mw: r + jnp.where(mw[0], mw[1], 0.0), zip(lt, (8.0, 4.0, 2.0, 1.0)), 0.0)
    return jnp.where(x < tops[PEER_TOPK - 1], float(PEER_TOPK), rank)


def _pack_rows(slabs, sub):
    out = slabs[-1]
    for k in range(len(slabs) - 2, -1, -1):
        out = jnp.where(sub == k, slabs[k], out)
    return out


def _route_kernel(x_ref, mod_ref, g_ref, wq_ref, keys_ref,
                  h_ref, r2_ref, e2_ref, n1_ref, w1_ref, s_ref, *, tt):
    x = x_ref[...]
    ms = jnp.mean(x * x, axis=-1, keepdims=True)
    y = x * lax.rsqrt(ms + RMS_EPS) * g_ref[...]
    h = (y * (1.0 + mod_ref[0, 4:5, :]) + mod_ref[0, 3:4, :]).astype(BF16)
    h_ref[...] = h
    q = jnp.dot(h, wq_ref[...], preferred_element_type=F32).astype(BF16)
    for hp in range(2 * PEER_HEADS):
        qhp = q[:, hp * D_HALF:(hp + 1) * D_HALF]
        s_ref[hp] = lax.dot_general(keys_ref[hp % 2], qhp, (((1,), (1,)), ((), ())),
                                    preferred_element_type=F32)

    sub = lax.broadcasted_iota(jnp.int32, (SUBLANES, LANES), 0)
    slabs = lambda m: [m[v * SUBLANES:(v + 1) * SUBLANES] for v in range(N_KEYS // SUBLANES)]
    one = lambda hit: jnp.where(hit, 1.0, 0.0)

    def per_chunk(idx, carry):
        head = idx // (tt // LANES)
        chunk = idx % (tt // LANES)
        cols = pl.ds(pl.multiple_of(chunk * LANES, LANES), LANES)
        s1 = s_ref[2 * head, :, cols]
        s2 = s_ref[2 * head + 1, :, cols]
        a = _top16_sorted(slabs(s1))
        bt = _top16_sorted(slabs(s2))
        a_lo, a_hi = _pack_rows(a[:SUBLANES], sub), _pack_rows(a[SUBLANES:], sub)
        b_hi = _pack_rows(bt[SUBLANES:], sub)
        cand = [a_lo + bt[0], a_hi + bt[0], a_lo + bt[1]]
        for r2 in range(2, SUBLANES):
            cand.append(jnp.where(sub < PEER_TOPK // (r2 + 1), a_lo + bt[r2], NEG))
        cand.append(b_hi + a[0])
        thr = _top16_sorted(cand, smallest_only=True)
        top = a[0] + bt[0]
        hits = [c >= thr for c in cand]
        z = None
        for c, hit in zip(cand, hits):
            zc = jnp.where(hit, jnp.exp(c - top), 0.0)
            z = zc if z is None else z + zc
        zinv = 0.5 / jnp.sum(z, axis=0, keepdims=True)
        cnt_lo = one(hits[0])
        for hit in hits[2:-1]:
            cnt_lo = cnt_lo + one(hit)
        tail = jnp.sum(one(hits[-1]), axis=0, keepdims=True)
        cnt_lo = cnt_lo + jnp.where(sub == 0, tail, 0.0)
        cnt_hi = one(hits[1])
        cnt = [jnp.broadcast_to((cnt_lo if r < SUBLANES else cnt_hi)[r % SUBLANES:r % SUBLANES + 1],
                                (SUBLANES, LANES)) for r in range(PEER_TOPK)]
        a0 = a[0][0:1]
        b0 = bt[0][0:1]
        for vp in range(N_KEYS // (2 * SUBLANES)):
            rk, ex = [], []
            for v in (2 * vp, 2 * vp + 1):
                rows = pl.ds(v * SUBLANES, SUBLANES)
                x1 = s1[v * SUBLANES:(v + 1) * SUBLANES]
                x2 = s2[v * SUBLANES:(v + 1) * SUBLANES]
                n1 = jnp.zeros_like(x1)
                for r in range(PEER_TOPK - 1, -1, -1):
                    n1 = jnp.where(x1 == a[r], cnt[r], n1)
                n1_ref[head, rows, cols] = n1
                w1_ref[head, rows, cols] = jnp.exp(x1 - a0) * zinv
                rk.append(_rank_among(x2, bt))
                ex.append(jnp.exp(x2 - b0))
            rows2 = pl.ds(vp * 2 * SUBLANES, 2 * SUBLANES)
            r2_ref[head, rows2, cols] = jnp.concatenate(rk, axis=0).astype(BF16)
            e2_ref[head, rows2, cols] = jnp.concatenate(ex, axis=0).astype(BF16)
        return carry

    lax.fori_loop(0, PEER_HEADS * (tt // LANES), per_chunk, 0)


def _route(x2, mod_l, norm_g, wq_b, keys_b, tt, s_len):
    t, d = x2.shape
    per_b = s_len // tt
    full = lambda shape: pl.BlockSpec(shape, lambda i: (0,) * len(shape))
    stat = lambda: pl.BlockSpec((PEER_HEADS, N_KEYS, tt), lambda i: (0, 0, i))
    return pl.pallas_call(
        functools.partial(_route_kernel, tt=tt),
        grid=(t // tt,),
        in_specs=[
            pl.BlockSpec((tt, d), lambda i: (i, 0)),
            pl.BlockSpec((1, N_MOD, d), lambda i: (i // per_b, 0, 0)),
            full((1, d)),
            full(wq_b.shape),
            full(keys_b.shape),
        ],
        out_specs=[pl.BlockSpec((tt, d), lambda i: (i, 0)), stat(), stat(), stat(), stat()],
        out_shape=[
            jax.ShapeDtypeStruct((t, d), BF16),
            jax.ShapeDtypeStruct((PEER_HEADS, N_KEYS, t), BF16),
            jax.ShapeDtypeStruct((PEER_HEADS, N_KEYS, t), BF16),
            jax.ShapeDtypeStruct((PEER_HEADS, N_KEYS, t), F32),
            jax.ShapeDtypeStruct((PEER_HEADS, N_KEYS, t), F32),
        ],
        scratch_shapes=[pltpu.VMEM((2 * PEER_HEADS, N_KEYS, tt), F32)],
        compiler_params=_cparams(("parallel",)),
        name="peer_route",
    )(x2, mod_l, norm_g, wq_b, keys_b)


def _peer_kernel(x_ref, h_ref, mod_ref, r2_ref, e2_ref, n1_ref, w1_ref, u_ref, vt_ref,
                 o_ref, acc_ref, act_ref, gate_ref, *, n_sub):
    e = pl.program_id(1)
    tt = h_ref.shape[0]

    @pl.when(e == 0)
    def _():
        acc_ref[...] = jnp.zeros_like(acc_ref)

    def row_tile(ref, hd, j):
        return jnp.broadcast_to(ref[hd, j:j + 1, :], (BF16_ROWS, tt)).astype(BF16)

    for j in range(n_sub):
        gate = None
        for hd in range(PEER_HEADS):
            n1 = row_tile(n1_ref, hd, j)
            w1 = row_tile(w1_ref, hd, j)
            term = jnp.where(r2_ref[hd] < n1[None], e2_ref[hd] * w1[None], jnp.zeros((), BF16))
            gate = term if gate is None else gate + term
        gate_ref[j * N_KEYS:(j + 1) * N_KEYS, :] = gate.reshape(N_KEYS, tt)
    a = lax.dot_general(u_ref[...], h_ref[...], (((1,), (1,)), ((), ())),
                        preferred_element_type=F32).astype(BF16)
    act_ref[...] = (a * gate_ref[...]) * (1.0 + lax.erf(a * (2.0 ** -0.5)))
    acc_ref[...] += jnp.dot(vt_ref[...], act_ref[...], preferred_element_type=F32)

    @pl.when(e == pl.num_programs(1) - 1)
    def _():
        o_ref[...] = x_ref[...] + mod_ref[0, 5:6, :] * acc_ref[...].T


def _peer(x2, h2, mod_l, r2, e2, n1, w1, u_b, vt_b, layer, tt, e_blk, s_len):
    t, d = x2.shape
    n_sub = e_blk // N_KEYS
    per_b = s_len // tt
    r2 = r2.reshape(PEER_HEADS, N_KEYS // BF16_ROWS, BF16_ROWS, t)
    e2 = e2.reshape(PEER_HEADS, N_KEYS // BF16_ROWS, BF16_ROWS, t)
    stat = lambda: pl.BlockSpec((PEER_HEADS, N_KEYS // BF16_ROWS, BF16_ROWS, tt), lambda i, e: (0, 0, 0, i))
    rows = lambda: pl.BlockSpec((PEER_HEADS, n_sub, tt), lambda i, e: (0, e, i))
    return pl.pallas_call(
        functools.partial(_peer_kernel, n_sub=n_sub),
        grid=(t // tt, N_EXPERTS // e_blk),
        in_specs=[
            pl.BlockSpec((tt, d), lambda i, e: (i, 0)),
            pl.BlockSpec((tt, d), lambda i, e: (i, 0)),
            pl.BlockSpec((1, N_MOD, d), lambda i, e: (i // per_b, 0, 0)),
            stat(), stat(), rows(), rows(),
            pl.BlockSpec((None, e_blk, d), lambda i, e: (layer, e, 0)),
            pl.BlockSpec((None, d, e_blk), lambda i, e: (layer, 0, e)),
        ],
        out_specs=pl.BlockSpec((tt, d), lambda i, e: (i, 0)),
        out_shape=jax.ShapeDtypeStruct((t, d), F32),
        scratch_shapes=[pltpu.VMEM((d, tt), F32), pltpu.VMEM((e_blk, tt), BF16),
                        pltpu.VMEM((e_blk, tt), BF16)],
        compiler_params=_cparams(("parallel", "arbitrary")),
        name="peer_dense",
    )(x2, h2, mod_l, r2, e2, n1, w1, u_b, vt_b)


def _pair_heads(w, axis):
    shape = w.shape
    w = w.reshape(shape[:axis] + (2, N_HEADS // 2, HEAD_DIM) + shape[axis + 1:])
    return jnp.swapaxes(w, axis, axis + 1).reshape(shape)


def _attn_bias():
    qi = jnp.arange(ATT_BLOCK)[:, None]
    sj = jnp.arange(2 * ATT_BLOCK)[None, :]
    dist = qi + ATT_BLOCK - sj
    ok = (dist >= 0) & (dist < ATT_BLOCK)
    slopes = jnp.exp2(-8.0 * jnp.arange(1, N_HEADS + 1, dtype=F32) / N_HEADS)
    bias = -slopes[:, None, None] * dist.astype(F32)[None]
    return jnp.where(ok[None], bias, NEG)


def kernel(x, c, w_ada, b_ada, norm1_g, norm2_g, w_in, pool_w, pool_scale, q_norm_g, k_norm_g,
           attn_sinks, mix_norm_g, w_out, peer_wq, peer_subkeys, peer_u, peer_v):
    b, s, d = x.shape
    depth = w_ada.shape[0]
    t = b * s
    ts = min(512, s)
    tt_route = min(512, s)
    tt_peer = min(1024, s)
    e_blk = 1024

    q0, k0 = POOL_WIDTH, POOL_WIDTH + ATTN_WIDTH
    w_in_b = jnp.concatenate([w_in[:, :, :q0], _pair_heads(w_in[:, :, q0:k0], 2), w_in[:, :, k0:]],
                             axis=2).astype(BF16)
    w_out_b = jnp.concatenate([w_out[:, :q0], _pair_heads(w_out[:, q0:], 1)], axis=1).astype(BF16)
    mix_g = jnp.concatenate([mix_norm_g[:, :q0], _pair_heads(mix_norm_g[:, q0:], 1)], axis=1)
    pool_w_b = pool_w.astype(BF16)
    wq_b = peer_wq.astype(BF16)
    keys_b = peer_subkeys.astype(BF16)
    u_b = peer_u.astype(BF16)
    vt_b = jnp.swapaxes(peer_v, 1, 2).astype(BF16)
    qg_t = jnp.tile(q_norm_g, (1, N_HEADS))
    kg_t = jnp.tile(k_norm_g, (1, N_KV_HEADS))
    lane_head = jnp.arange(ATTN_WIDTH) // HEAD_DIM
    bd = jnp.where(lane_head[:, None] == lane_head[None, :], 1.0 / HEAD_DIM, 0.0).astype(BF16)
    bias = _attn_bias()

    mod = _adaln_mod(c, w_ada, b_ada).reshape(depth, b, N_MOD, d)

    for l in range(depth):
        p, q, k, v = _inproj(x, mod[l], norm1_g[l:l + 1], w_in_b[l], qg_t[l:l + 1], kg_t[l:l + 1], bd, ts)
        x = _mixer(x, p, q, k, v, mod[l], attn_sinks[l], bias, pool_w_b[l], pool_scale[l:l + 1],
                   mix_g[l:l + 1], w_out_b[l], ts)
        x2 = x.reshape(t, d)
        h2, r2, e2, n1, w1 = _route(x2, mod[l], norm2_g[l:l + 1], wq_b[l], keys_b[l], tt_route, s)
        x = _peer(x2, h2, mod[l], r2, e2, n1, w1, u_b, vt_b, l, tt_peer, e_blk, s).reshape(b, s, d)
    return x
```

```python
import functools

import jax
import jax.numpy as jnp
from jax import lax
from jax.experimental import pallas as pl
from jax.experimental.pallas import tpu as pltpu

F32 = jnp.float32
BF16 = jnp.bfloat16

D_MODEL = 1024
POOL_WIDTH = 512
POOL_WINDOWS = (2, 4, 8, 16)
POOL_GROUP = 128
POOL_HALO = 16
ATTN_WIDTH = 512
HEAD_DIM = 64
N_HEADS = 8
N_KV_HEADS = 2
KV_WIDTH = N_KV_HEADS * HEAD_DIM
ATT_BLOCK = 128
N_KEYS = 128
N_EXPERTS = N_KEYS * N_KEYS
PEER_HEADS = 8
PEER_TOPK = 16
D_HALF = 128
RMS_EPS = 1e-6
N_MOD = 6
NEG = -1e30

LANES = 128
SUBLANES = 8
BF16_ROWS = 2 * SUBLANES
VMEM_LIMIT = 56 * 1024 * 1024


def _cparams(sem):
    return pltpu.CompilerParams(dimension_semantics=sem, vmem_limit_bytes=VMEM_LIMIT)


def _mod_kernel(c_ref, w_ref, b_ref, o_ref):
    c = c_ref[...]
    cond = c * jax.nn.sigmoid(c)
    o_ref[0] = jnp.dot(cond, w_ref[0], preferred_element_type=F32,
                       precision=lax.Precision.HIGHEST) + b_ref[0]


def _adaln_mod(c, w_ada, b_ada, tn=1536):
    depth, d, n = w_ada.shape
    b = c.shape[0]
    return pl.pallas_call(
        _mod_kernel,
        grid=(depth, n // tn),
        in_specs=[
            pl.BlockSpec((b, d), lambda l, j: (0, 0)),
            pl.BlockSpec((1, d, tn), lambda l, j: (l, 0, j)),
            pl.BlockSpec((1, 1, tn), lambda l, j: (l, 0, j)),
        ],
        out_specs=pl.BlockSpec((1, b, tn), lambda l, j: (l, 0, j)),
        out_shape=jax.ShapeDtypeStruct((depth, b, n), F32),
        compiler_params=_cparams(("parallel", "parallel")),
        name="adaln_mod",
    )(c, w_ada, b_ada.reshape(depth, 1, n))


def _head_mean(t, bd):
    return jnp.dot(t.astype(BF16), bd, preferred_element_type=F32)


def _inproj_kernel(x_ref, mod_ref, g_ref, w_ref, qg_ref, kg_ref, bd_ref,
                   p_ref, q_ref, k_ref, v_ref):
    x = x_ref[0]
    ms = jnp.mean(x * x, axis=-1, keepdims=True)
    y = x * lax.rsqrt(ms + RMS_EPS) * g_ref[...]
    h = y * (1.0 + mod_ref[0, 1:2, :]) + mod_ref[0, 0:1, :]
    proj = jnp.dot(h.astype(BF16), w_ref[...], preferred_element_type=F32)
    q0, k0, v0 = POOL_WIDTH, POOL_WIDTH + ATTN_WIDTH, POOL_WIDTH + ATTN_WIDTH + KV_WIDTH
    p_ref[0] = proj[:, :q0]
    q = proj[:, q0:k0]
    k = proj[:, k0:v0]
    bd = bd_ref[...]
    q_ms = _head_mean(q * q, bd)
    k_ms = _head_mean(k * k, bd[:KV_WIDTH, :KV_WIDTH])
    qn = q * lax.rsqrt(q_ms + RMS_EPS) * qg_ref[...]
    kn = k * lax.rsqrt(k_ms + RMS_EPS) * kg_ref[...]
    q_ref[0] = (qn * (HEAD_DIM ** -0.5)).astype(BF16)
    k_ref[0] = kn.astype(BF16)
    v_ref[0] = proj[:, v0:].astype(BF16)


def _inproj(x, mod_l, norm_g, w_in_b, qg_t, kg_t, bd, ts):
    b, s, d = x.shape
    n_in = w_in_b.shape[1]
    full = lambda shape: pl.BlockSpec(shape, lambda bi, i: (0,) * len(shape))
    return pl.pallas_call(
        _inproj_kernel,
        grid=(b, s // ts),
        in_specs=[
            pl.BlockSpec((1, ts, d), lambda bi, i: (bi, i, 0)),
            pl.BlockSpec((1, N_MOD, d), lambda bi, i: (bi, 0, 0)),
            full((1, d)),
            full((d, n_in)),
            full((1, ATTN_WIDTH)),
            full((1, KV_WIDTH)),
            full((ATTN_WIDTH, ATTN_WIDTH)),
        ],
        out_specs=[
            pl.BlockSpec((1, ts, POOL_WIDTH), lambda bi, i: (bi, i, 0)),
            pl.BlockSpec((1, ts, ATTN_WIDTH), lambda bi, i: (bi, i, 0)),
            pl.BlockSpec((1, ts, KV_WIDTH), lambda bi, i: (bi, i, 0)),
            pl.BlockSpec((1, ts, KV_WIDTH), lambda bi, i: (bi, i, 0)),
        ],
        out_shape=[
            jax.ShapeDtypeStruct((b, s, POOL_WIDTH), F32),
            jax.ShapeDtypeStruct((b, s, ATTN_WIDTH), BF16),
            jax.ShapeDtypeStruct((b, s, KV_WIDTH), BF16),
            jax.ShapeDtypeStruct((b, s, KV_WIDTH), BF16),
        ],
        compiler_params=_cparams(("parallel", "parallel")),
        name="inproj",
    )(x, mod_l, norm_g, w_in_b, qg_t, kg_t, bd)


def _mixer_kernel(sink_ref, x_ref, p_ref, pp_ref, q_ref, k_ref, kp_ref, v_ref, vp_ref,
                  mod_ref, bias_ref, pw_ref, ps_ref, mg_ref, wo_ref, o_ref,
                  kext_ref, vext_ref, attn_ref, pool_ref, *, ts):
    i = pl.program_id(1)
    nblk = ts // ATT_BLOCK

    has_prev = (i > 0).astype(F32)
    pext = jnp.concatenate([pp_ref[0] * has_prev, p_ref[0]], axis=0)
    pos1 = (lax.broadcasted_iota(jnp.int32, (ts, 1), 0) + i * ts + 1).astype(F32)
    for g, w in enumerate(POOL_WINDOWS):
        xg = pext[:, g * POOL_GROUP:(g + 1) * POOL_GROUP]
        sg = xg
        step = 1
        while step < w:
            sg = sg + pltpu.roll(sg, step, 0)
            step *= 2
        pooled = sg[POOL_HALO:] / jnp.minimum(pos1, float(w)) - xg[POOL_HALO:]
        mixed = jnp.dot(pooled.astype(BF16), pw_ref[g], preferred_element_type=F32)
        pool_ref[:, g * POOL_GROUP:(g + 1) * POOL_GROUP] = (
            mixed * ps_ref[:, g * POOL_GROUP:(g + 1) * POOL_GROUP])

    kext_ref[0:ATT_BLOCK] = kp_ref[0]
    kext_ref[ATT_BLOCK:] = k_ref[0]
    vext_ref[0:ATT_BLOCK] = vp_ref[0]
    vext_ref[ATT_BLOCK:] = v_ref[0]
    lane = lax.broadcasted_iota(jnp.int32, (ATT_BLOCK, LANES), 1)
    low_half = lane < HEAD_DIM
    col = lax.broadcasted_iota(jnp.int32, (1, 2 * ATT_BLOCK), 1)
    zero_b = jnp.zeros((ATT_BLOCK, LANES), BF16)
    for j in range(nblk):
        kb = kext_ref[j * ATT_BLOCK:(j + 2) * ATT_BLOCK]
        vb = vext_ref[j * ATT_BLOCK:(j + 2) * ATT_BLOCK]
        qs = q_ref[0, j * ATT_BLOCK:(j + 1) * ATT_BLOCK, :]
        stack = []
        for r in range(N_HEADS):
            slab = qs[:, (r % 4) * LANES:(r % 4 + 1) * LANES]
            keep = low_half if r < 4 else jnp.logical_not(low_half)
            stack.append(jnp.where(keep, slab, zero_b))
        qst = jnp.concatenate(stack, axis=0)
        scores = lax.dot_general(qst, kb, (((1,), (1,)), ((), ())),
                                 preferred_element_type=F32)
        first = jnp.logical_and(i == 0, j == 0)
        pen = jnp.where(jnp.logical_and(first, col < ATT_BLOCK), NEG, 0.0)
        probs = []
        inv = []
        for r in range(N_HEADS):
            sr = scores[r * ATT_BLOCK:(r + 1) * ATT_BLOCK] + bias_ref[r] + pen
            sink = sink_ref[r]
            m = jnp.maximum(jnp.max(sr, axis=-1, keepdims=True), sink)
            e = jnp.exp(sr - m)
            den = jnp.sum(e, axis=-1, keepdims=True) + jnp.exp(sink - m)
            probs.append(e.astype(BF16))
            inv.append(1.0 / den)
        pst = jnp.concatenate(probs, axis=0)
        pv = jnp.dot(pst, vb, preferred_element_type=F32)
        for m4 in range(4):
            lo = pv[m4 * ATT_BLOCK:(m4 + 1) * ATT_BLOCK] * inv[m4]
            hi = pv[(m4 + 4) * ATT_BLOCK:(m4 + 5) * ATT_BLOCK] * inv[m4 + 4]
            attn_ref[j * ATT_BLOCK:(j + 1) * ATT_BLOCK, m4 * LANES:(m4 + 1) * LANES] = (
                jnp.where(low_half, lo, hi))

    ya = pool_ref[...]
    yb = attn_ref[...]
    ya = ya * lax.rsqrt(jnp.mean(ya * ya, axis=-1, keepdims=True) + RMS_EPS) * mg_ref[:, :POOL_WIDTH]
    yb = yb * lax.rsqrt(jnp.mean(yb * yb, axis=-1, keepdims=True) + RMS_EPS) * mg_ref[:, POOL_WIDTH:]
    out = (jnp.dot(ya.astype(BF16), wo_ref[:POOL_WIDTH, :], preferred_element_type=F32)
           + jnp.dot(yb.astype(BF16), wo_ref[POOL_WIDTH:, :], preferred_element_type=F32))
    o_ref[0] = x_ref[0] + mod_ref[0, 2:3, :] * out


def _mixer(x, p, q, k, v, mod_l, sinks, bias, pool_w_b, pool_scale, mix_g, w_out_b, ts):
    b, s, d = x.shape
    nb = ts // ATT_BLOCK
    nh = ts // POOL_HALO
    full = lambda shape: pl.BlockSpec(shape, lambda bi, i: (0,) * len(shape))
    row = lambda w: pl.BlockSpec((1, ts, w), lambda bi, i: (bi, i, 0))
    prev_blk = pl.BlockSpec((1, ATT_BLOCK, KV_WIDTH),
                            lambda bi, i: (bi, jnp.maximum(i * nb - 1, 0), 0))
    return pl.pallas_call(
        functools.partial(_mixer_kernel, ts=ts),
        grid=(b, s // ts),
        in_specs=[
            pl.BlockSpec(memory_space=pltpu.SMEM),
            row(d),
            row(POOL_WIDTH),
            pl.BlockSpec((1, POOL_HALO, POOL_WIDTH),
                         lambda bi, i: (bi, jnp.maximum(i * nh - 1, 0), 0)),
            row(ATTN_WIDTH),
            row(KV_WIDTH), prev_blk,
            row(KV_WIDTH), prev_blk,
            pl.BlockSpec((1, N_MOD, d), lambda bi, i: (bi, 0, 0)),
            full((N_HEADS, ATT_BLOCK, 2 * ATT_BLOCK)),
            full((len(POOL_WINDOWS), POOL_GROUP, POOL_GROUP)),
            full((1, POOL_WIDTH)),
            full((1, d)),
            full((d, d)),
        ],
        out_specs=row(d),
        out_shape=jax.ShapeDtypeStruct((b, s, d), F32),
        scratch_shapes=[
            pltpu.VMEM((ts + ATT_BLOCK, KV_WIDTH), BF16),
            pltpu.VMEM((ts + ATT_BLOCK, KV_WIDTH), BF16),
            pltpu.VMEM((ts, ATTN_WIDTH), F32),
            pltpu.VMEM((ts, POOL_WIDTH), F32),
        ],
        compiler_params=_cparams(("parallel", "parallel")),
        name="mixer",
    )(sinks, x, p, p, q, k, k, v, v, mod_l, bias, pool_w_b, pool_scale, mix_g, w_out_b)


def _oddeven_merge(lo, hi, r):
    step = r * 2
    if step < hi - lo:
        yield from _oddeven_merge(lo, hi, step)
        yield from _oddeven_merge(lo + r, hi, step)
        yield from [(i, i + r) for i in range(lo + r, hi - r, step)]
    else:
        yield (lo, lo + r)


def _oddeven_sort(lo, hi):
    if hi - lo >= 1:
        mid = lo + (hi - lo) // 2
        yield from _oddeven_sort(lo, mid)
        yield from _oddeven_sort(mid + 1, hi)
        yield from _oddeven_merge(lo, hi, 1)


_SORT16 = tuple(_oddeven_sort(0, PEER_TOPK - 1))
_BITONIC16 = tuple((i, i + st) for st in (8, 4, 2, 1) for i in range(PEER_TOPK) if not i & st)


def _compare_exchange(x, net):
    for i, j in net:
        if j < len(x):
            x[i], x[j] = jnp.maximum(x[i], x[j]), jnp.minimum(x[i], x[j])
    return x


def _top16_sorted(rows, smallest_only=False):
    x = _compare_exchange(list(rows), _SORT16)
    x = x + [jnp.full_like(x[0], NEG)] * (PEER_TOPK - len(x))
    for shift in (4, 2, 1):
        y = [pltpu.roll(v, shift, 0) for v in x]
        x = [jnp.maximum(x[v], y[PEER_TOPK - 1 - v]) for v in range(PEER_TOPK)]
        if smallest_only and shift == 1:
            return functools.reduce(jnp.minimum, x)
        x = _compare_exchange(x, _BITONIC16)
    return x


def _rank_among(x, tops):
    lt = [x < tops[7]]
    pick = lambda lo, hi, m: jnp.where(m, hi, lo)
    t4 = pick(tops[3], tops[11], lt[0])
    lt.append(x < t4)
    t2 = pick(pick(tops[1], tops[9], lt[0]), pick(tops[5], tops[13], lt[0]), lt[1])
    lt.append(x < t2)
    t1 = pick(pick(pick(tops[0], tops[8], lt[0]), pick(tops[4], tops[12], lt[0]), lt[1]),
              pick(pick(tops[2], tops[10], lt[0]), pick(tops[6], tops[14], lt[0]), lt[1]), lt[2])
    lt.append(x < t1)
    rank = functools.reduce(lambda r, mw: r + jnp.where(mw[0], mw[1], 0.0), zip(lt, (8.0, 4.0, 2.0, 1.0)), 0.0)
    return jnp.where(x < tops[PEER_TOPK - 1], float(PEER_TOPK), rank)


def _pack_rows(slabs, sub):
    out = slabs[-1]
    for k in range(len(slabs) - 2, -1, -1):
        out = jnp.where(sub == k, slabs[k], out)
    return out


def _route_kernel(x_ref, mod_ref, g_ref, wq_ref, keys_ref,
                  h_ref, r2_ref, e2_ref, n1_ref, w1_ref, s_ref, *, tt):
    x = x_ref[...]
    ms = jnp.mean(x * x, axis=-1, keepdims=True)
    y = x * lax.rsqrt(ms + RMS_EPS) * g_ref[...]
    h = (y * (1.0 + mod_ref[0, 4:5, :]) + mod_ref[0, 3:4, :]).astype(BF16)
    h_ref[...] = h
    q = jnp.dot(h, wq_ref[...], preferred_element_type=F32).astype(BF16)
    for hp in range(2 * PEER_HEADS):
        qhp = q[:, hp * D_HALF:(hp + 1) * D_HALF]
        s_ref[hp] = lax.dot_general(keys_ref[hp % 2], qhp, (((1,), (1,)), ((), ())),
                                    preferred_element_type=F32)

    sub = lax.broadcasted_iota(jnp.int32, (SUBLANES, LANES), 0)
    slabs = lambda m: [m[v * SUBLANES:(v + 1) * SUBLANES] for v in range(N_KEYS // SUBLANES)]
    one = lambda hit: jnp.where(hit, 1.0, 0.0)

    def per_chunk(idx, carry):
        head = idx // (tt // LANES)
        chunk = idx % (tt // LANES)
        cols = pl.ds(pl.multiple_of(chunk * LANES, LANES), LANES)
        s1 = s_ref[2 * head, :, cols]
        s2 = s_ref[2 * head + 1, :, cols]
        a = _top16_sorted(slabs(s1))
        bt = _top16_sorted(slabs(s2))
        a_lo, a_hi = _pack_rows(a[:SUBLANES], sub), _pack_rows(a[SUBLANES:], sub)
        b_hi = _pack_rows(bt[SUBLANES:], sub)
        cand = [a_lo + bt[0], a_hi + bt[0], a_lo + bt[1]]
        for r2 in range(2, SUBLANES):
            cand.append(jnp.where(sub < PEER_TOPK // (r2 + 1), a_lo + bt[r2], NEG))
        cand.append(b_hi + a[0])
        thr = _top16_sorted(cand, smallest_only=True)
        top = a[0] + bt[0]
        hits = [c >= thr for c in cand]
        z = None
        for c, hit in zip(cand, hits):
            zc = jnp.where(hit, jnp.exp(c - top), 0.0)
            z = zc if z is None else z + zc
        zinv = 0.5 / jnp.sum(z, axis=0, keepdims=True)
        cnt_lo = one(hits[0])
        for hit in hits[2:-1]:
            cnt_lo = cnt_lo + one(hit)
        tail = jnp.sum(one(hits[-1]), axis=0, keepdims=True)
        cnt_lo = cnt_lo + jnp.where(sub == 0, tail, 0.0)
        cnt_hi = one(hits[1])
        cnt = [jnp.broadcast_to((cnt_lo if r < SUBLANES else cnt_hi)[r % SUBLANES:r % SUBLANES + 1],
                                (SUBLANES, LANES)) for r in range(PEER_TOPK)]
        a0 = a[0][0:1]
        b0 = bt[0][0:1]
        for vp in range(N_KEYS // (2 * SUBLANES)):
            rk, ex = [], []
            for v in (2 * vp, 2 * vp + 1):
                rows = pl.ds(v * SUBLANES, SUBLANES)
                x1 = s1[v * SUBLANES:(v + 1) * SUBLANES]
                x2 = s2[v * SUBLANES:(v + 1) * SUBLANES]
                n1 = jnp.zeros_like(x1)
                for r in range(PEER_TOPK - 1, -1, -1):
                    n1 = jnp.where(x1 == a[r], cnt[r], n1)
                n1_ref[head, rows, cols] = n1
                w1_ref[head, rows, cols] = jnp.exp(x1 - a0) * zinv
                rk.append(_rank_among(x2, bt))
                ex.append(jnp.exp(x2 - b0))
            rows2 = pl.ds(vp * 2 * SUBLANES, 2 * SUBLANES)
            r2_ref[head, rows2, cols] = jnp.concatenate(rk, axis=0).astype(BF16)
            e2_ref[head, rows2, cols] = jnp.concatenate(ex, axis=0).astype(BF16)
        return carry

    lax.fori_loop(0, PEER_HEADS * (tt // LANES), per_chunk, 0)


def _route(x2, mod_l, norm_g, wq_b, keys_b, tt, s_len):
    t, d = x2.shape
    per_b = s_len // tt
    full = lambda shape: pl.BlockSpec(shape, lambda i: (0,) * len(shape))
    stat = lambda: pl.BlockSpec((PEER_HEADS, N_KEYS, tt), lambda i: (0, 0, i))
    return pl.pallas_call(
        functools.partial(_route_kernel, tt=tt),
        grid=(t // tt,),
        in_specs=[
            pl.BlockSpec((tt, d), lambda i: (i, 0)),
            pl.BlockSpec((1, N_MOD, d), lambda i: (i // per_b, 0, 0)),
            full((1, d)),
            full(wq_b.shape),
            full(keys_b.shape),
        ],
        out_specs=[pl.BlockSpec((tt, d), lambda i: (i, 0)), stat(), stat(), stat(), stat()],
        out_shape=[
            jax.ShapeDtypeStruct((t, d), BF16),
            jax.ShapeDtypeStruct((PEER_HEADS, N_KEYS, t), BF16),
            jax.ShapeDtypeStruct((PEER_HEADS, N_KEYS, t), BF16),
            jax.ShapeDtypeStruct((PEER_HEADS, N_KEYS, t), F32),
            jax.ShapeDtypeStruct((PEER_HEADS, N_KEYS, t), F32),
        ],
        scratch_shapes=[pltpu.VMEM((2 * PEER_HEADS, N_KEYS, tt), F32)],
        compiler_params=_cparams(("parallel",)),
        name="peer_route",
    )(x2, mod_l, norm_g, wq_b, keys_b)


def _peer_kernel(x_ref, h_ref, mod_ref, r2_ref, e2_ref, n1_ref, w1_ref, u_ref, vt_ref,
                 o_ref, acc_ref, act_ref, gate_ref, *, n_sub):
    e = pl.program_id(1)
    tt = h_ref.shape[0]

    @pl.when(e == 0)
    def _():
        acc_ref[...] = jnp.zeros_like(acc_ref)

    def row_tile(ref, hd, j):
        return jnp.broadcast_to(ref[hd, j:j + 1, :], (BF16_ROWS, tt)).astype(BF16)

    for j in range(n_sub):
        gate = None
        for hd in range(PEER_HEADS):
            n1 = row_tile(n1_ref, hd, j)
            w1 = row_tile(w1_ref, hd, j)
            term = jnp.where(r2_ref[hd] < n1[None], e2_ref[hd] * w1[None], jnp.zeros((), BF16))
            gate = term if gate is None else gate + term
        gate_ref[j * N_KEYS:(j + 1) * N_KEYS, :] = gate.reshape(N_KEYS, tt)
    a = lax.dot_general(u_ref[...], h_ref[...], (((1,), (1,)), ((), ())),
                        preferred_element_type=F32).astype(BF16)
    act_ref[...] = (a * gate_ref[...]) * (1.0 + lax.erf(a * (2.0 ** -0.5)))
    acc_ref[...] += jnp.dot(vt_ref[...], act_ref[...], preferred_element_type=F32)

    @pl.when(e == pl.num_programs(1) - 1)
    def _():
        o_ref[...] = x_ref[...] + mod_ref[0, 5:6, :] * acc_ref[...].T


def _peer(x2, h2, mod_l, r2, e2, n1, w1, u_b, vt_b, layer, tt, e_blk, s_len):
    t, d = x2.shape
    n_sub = e_blk // N_KEYS
    per_b = s_len // tt
    r2 = r2.reshape(PEER_HEADS, N_KEYS // BF16_ROWS, BF16_ROWS, t)
    e2 = e2.reshape(PEER_HEADS, N_KEYS // BF16_ROWS, BF16_ROWS, t)
    stat = lambda: pl.BlockSpec((PEER_HEADS, N_KEYS // BF16_ROWS, BF16_ROWS, tt), lambda i, e: (0, 0, 0, i))
    rows = lambda: pl.BlockSpec((PEER_HEADS, n_sub, tt), lambda i, e: (0, e, i))
    return pl.pallas_call(
        functools.partial(_peer_kernel, n_sub=n_sub),
        grid=(t // tt, N_EXPERTS // e_blk),
        in_specs=[
            pl.BlockSpec((tt, d), lambda i, e: (i, 0)),
            pl.BlockSpec((tt, d), lambda i, e: (i, 0)),
            pl.BlockSpec((1, N_MOD, d), lambda i, e: (i // per_b, 0, 0)),
            stat(), stat(), rows(), rows(),
            pl.BlockSpec((None, e_blk, d), lambda i, e: (layer, e, 0)),
            pl.BlockSpec((None, d, e_blk), lambda i, e: (layer, 0, e)),
        ],
        out_specs=pl.BlockSpec((tt, d), lambda i, e: (i, 0)),
        out_shape=jax.ShapeDtypeStruct((t, d), F32),
        scratch_shapes=[pltpu.VMEM((d, tt), F32), pltpu.VMEM((e_blk, tt), BF16),
                        pltpu.VMEM((e_blk, tt), BF16)],
        compiler_params=_cparams(("parallel", "arbitrary")),
        name="peer_dense",
    )(x2, h2, mod_l, r2, e2, n1, w1, u_b, vt_b)


def _pair_heads(w, axis):
    shape = w.shape
    w = w.reshape(shape[:axis] + (2, N_HEADS // 2, HEAD_DIM) + shape[axis + 1:])
    return jnp.swapaxes(w, axis, axis + 1).reshape(shape)


def _attn_bias():
    qi = jnp.arange(ATT_BLOCK)[:, None]
    sj = jnp.arange(2 * ATT_BLOCK)[None, :]
    dist = qi + ATT_BLOCK - sj
    ok = (dist >= 0) & (dist < ATT_BLOCK)
    slopes = jnp.exp2(-8.0 * jnp.arange(1, N_HEADS + 1, dtype=F32) / N_HEADS)
    bias = -slopes[:, None, None] * dist.astype(F32)[None]
    return jnp.where(ok[None], bias, NEG)


def kernel(x, c, w_ada, b_ada, norm1_g, norm2_g, w_in, pool_w, pool_scale, q_norm_g, k_norm_g,
           attn_sinks, mix_norm_g, w_out, peer_wq, peer_subkeys, peer_u, peer_v):
    b, s, d = x.shape
    depth = w_ada.shape[0]
    t = b * s
    ts = min(1024, s)
    tt_route = min(512, s)
    tt_peer = min(1024, s)
    e_blk = 1024

    q0, k0 = POOL_WIDTH, POOL_WIDTH + ATTN_WIDTH
    w_in_b = jnp.concatenate([w_in[:, :, :q0], _pair_heads(w_in[:, :, q0:k0], 2), w_in[:, :, k0:]],
                             axis=2).astype(BF16)
    w_out_b = jnp.concatenate([w_out[:, :q0], _pair_heads(w_out[:, q0:], 1)], axis=1).astype(BF16)
    mix_g = jnp.concatenate([mix_norm_g[:, :q0], _pair_heads(mix_norm_g[:, q0:], 1)], axis=1)
    pool_w_b = pool_w.astype(BF16)
    wq_b = peer_wq.astype(BF16)
    keys_b = peer_subkeys.astype(BF16)
    u_b = peer_u.astype(BF16)
    vt_b = jnp.swapaxes(peer_v, 1, 2).astype(BF16)
    qg_t = jnp.tile(q_norm_g, (1, N_HEADS))
    kg_t = jnp.tile(k_norm_g, (1, N_KV_HEADS))
    lane_head = jnp.arange(ATTN_WIDTH) // HEAD_DIM
    bd = jnp.where(lane_head[:, None] == lane_head[None, :], 1.0 / HEAD_DIM, 0.0).astype(BF16)
    bias = _attn_bias()

    mod = _adaln_mod(c, w_ada, b_ada).reshape(depth, b, N_MOD, d)

    for l in range(depth):
        p, q, k, v = _inproj(x, mod[l], norm1_g[l:l + 1], w_in_b[l], qg_t[l:l + 1], kg_t[l:l + 1], bd, ts)
        x = _mixer(x, p, q, k, v, mod[l], attn_sinks[l], bias, pool_w_b[l], pool_scale[l:l + 1],
                   mix_g[l:l + 1], w_out_b[l], ts)
        x2 = x.reshape(t, d)
        h2, r2, e2, n1, w1 = _route(x2, mod[l], norm2_g[l:l + 1], wq_b[l], keys_b[l], tt_route, s)
        x = _peer(x2, h2, mod[l], r2, e2, n1, w1, u_b, vt_b, l, tt_peer, e_blk, s).reshape(b, s, d)
    return x
```

```python
import functools

import jax
import jax.numpy as jnp
from jax import lax
from jax.experimental import pallas as pl
from jax.experimental.pallas import tpu as pltpu

F32 = jnp.float32
BF16 = jnp.bfloat16

D_MODEL = 1024
POOL_WIDTH = 512
POOL_WINDOWS = (2, 4, 8, 16)
POOL_GROUP = 128
POOL_HALO = 16
ATTN_WIDTH = 512
HEAD_DIM = 64
N_HEADS = 8
N_KV_HEADS = 2
KV_WIDTH = N_KV_HEADS * HEAD_DIM
ATT_BLOCK = 128
N_KEYS = 128
N_EXPERTS = N_KEYS * N_KEYS
PEER_HEADS = 8
PEER_TOPK = 16
D_HALF = 128
RMS_EPS = 1e-6
N_MOD = 6
NEG = -1e30

LANES = 128
SUBLANES = 8
BF16_ROWS = 2 * SUBLANES
VMEM_LIMIT = 56 * 1024 * 1024


def _cparams(sem):
    return pltpu.CompilerParams(dimension_semantics=sem, vmem_limit_bytes=VMEM_LIMIT)


def _mod_kernel(c_ref, w_ref, b_ref, o_ref):
    c = c_ref[...]
    cond = c * jax.nn.sigmoid(c)
    o_ref[0] = jnp.dot(cond, w_ref[0], preferred_element_type=F32,
                       precision=lax.Precision.HIGHEST) + b_ref[0]


def _adaln_mod(c, w_ada, b_ada, tn=3072):
    depth, d, n = w_ada.shape
    b = c.shape[0]
    return pl.pallas_call(
        _mod_kernel,
        grid=(depth, n // tn),
        in_specs=[
            pl.BlockSpec((b, d), lambda l, j: (0, 0)),
            pl.BlockSpec((1, d, tn), lambda l, j: (l, 0, j)),
            pl.BlockSpec((1, 1, tn), lambda l, j: (l, 0, j)),
        ],
        out_specs=pl.BlockSpec((1, b, tn), lambda l, j: (l, 0, j)),
        out_shape=jax.ShapeDtypeStruct((depth, b, n), F32),
        compiler_params=_cparams(("parallel", "parallel")),
        name="adaln_mod",
    )(c, w_ada, b_ada.reshape(depth, 1, n))


def _head_mean(t, bd):
    return jnp.dot(t.astype(BF16), bd, preferred_element_type=F32)


def _inproj_kernel(x_ref, mod_ref, g_ref, w_ref, qg_ref, kg_ref, bd_ref,
                   p_ref, q_ref, k_ref, v_ref):
    x = x_ref[0]
    ms = jnp.mean(x * x, axis=-1, keepdims=True)
    y = x * lax.rsqrt(ms + RMS_EPS) * g_ref[...]
    h = y * (1.0 + mod_ref[0, 1:2, :]) + mod_ref[0, 0:1, :]
    proj = jnp.dot(h.astype(BF16), w_ref[...], preferred_element_type=F32)
    q0, k0, v0 = POOL_WIDTH, POOL_WIDTH + ATTN_WIDTH, POOL_WIDTH + ATTN_WIDTH + KV_WIDTH
    p_ref[0] = proj[:, :q0]
    q = proj[:, q0:k0]
    k = proj[:, k0:v0]
    bd = bd_ref[...]
    q_ms = _head_mean(q * q, bd)
    k_ms = _head_mean(k * k, bd[:KV_WIDTH, :KV_WIDTH])
    qn = q * lax.rsqrt(q_ms + RMS_EPS) * qg_ref[...]
    kn = k * lax.rsqrt(k_ms + RMS_EPS) * kg_ref[...]
    q_ref[0] = (qn * (HEAD_DIM ** -0.5)).astype(BF16)
    k_ref[0] = kn.astype(BF16)
    v_ref[0] = proj[:, v0:].astype(BF16)


def _inproj(x, mod_l, norm_g, w_in_b, qg_t, kg_t, bd, ts):
    b, s, d = x.shape
    n_in = w_in_b.shape[1]
    full = lambda shape: pl.BlockSpec(shape, lambda bi, i: (0,) * len(shape))
    return pl.pallas_call(
        _inproj_kernel,
        grid=(b, s // ts),
        in_specs=[
            pl.BlockSpec((1, ts, d), lambda bi, i: (bi, i, 0)),
            pl.BlockSpec((1, N_MOD, d), lambda bi, i: (bi, 0, 0)),
            full((1, d)),
            full((d, n_in)),
            full((1, ATTN_WIDTH)),
            full((1, KV_WIDTH)),
            full((ATTN_WIDTH, ATTN_WIDTH)),
        ],
        out_specs=[
            pl.BlockSpec((1, ts, POOL_WIDTH), lambda bi, i: (bi, i, 0)),
            pl.BlockSpec((1, ts, ATTN_WIDTH), lambda bi, i: (bi, i, 0)),
            pl.BlockSpec((1, ts, KV_WIDTH), lambda bi, i: (bi, i, 0)),
            pl.BlockSpec((1, ts, KV_WIDTH), lambda bi, i: (bi, i, 0)),
        ],
        out_shape=[
            jax.ShapeDtypeStruct((b, s, POOL_WIDTH), F32),
            jax.ShapeDtypeStruct((b, s, ATTN_WIDTH), BF16),
            jax.ShapeDtypeStruct((b, s, KV_WIDTH), BF16),
            jax.ShapeDtypeStruct((b, s, KV_WIDTH), BF16),
        ],
        compiler_params=_cparams(("parallel", "parallel")),
        name="inproj",
    )(x, mod_l, norm_g, w_in_b, qg_t, kg_t, bd)


def _mixer_kernel(sink_ref, x_ref, p_ref, pp_ref, q_ref, k_ref, kp_ref, v_ref, vp_ref,
                  mod_ref, bias_ref, pw_ref, ps_ref, mg_ref, wo_ref, o_ref,
                  kext_ref, vext_ref, attn_ref, pool_ref, *, ts):
    i = pl.program_id(1)
    nblk = ts // ATT_BLOCK

    has_prev = (i > 0).astype(F32)
    pext = jnp.concatenate([pp_ref[0] * has_prev, p_ref[0]], axis=0)
    pos1 = (lax.broadcasted_iota(jnp.int32, (ts, 1), 0) + i * ts + 1).astype(F32)
    for g, w in enumerate(POOL_WINDOWS):
        xg = pext[:, g * POOL_GROUP:(g + 1) * POOL_GROUP]
        sg = xg
        step = 1
        while step < w:
            sg = sg + pltpu.roll(sg, step, 0)
            step *= 2
        pooled = sg[POOL_HALO:] / jnp.minimum(pos1, float(w)) - xg[POOL_HALO:]
        mixed = jnp.dot(pooled.astype(BF16), pw_ref[g], preferred_element_type=F32)
        pool_ref[:, g * POOL_GROUP:(g + 1) * POOL_GROUP] = (
            mixed * ps_ref[:, g * POOL_GROUP:(g + 1) * POOL_GROUP])

    kext_ref[0:ATT_BLOCK] = kp_ref[0]
    kext_ref[ATT_BLOCK:] = k_ref[0]
    vext_ref[0:ATT_BLOCK] = vp_ref[0]
    vext_ref[ATT_BLOCK:] = v_ref[0]
    lane = lax.broadcasted_iota(jnp.int32, (ATT_BLOCK, LANES), 1)
    low_half = lane < HEAD_DIM
    col = lax.broadcasted_iota(jnp.int32, (1, 2 * ATT_BLOCK), 1)
    zero_b = jnp.zeros((ATT_BLOCK, LANES), BF16)
    for j in range(nblk):
        kb = kext_ref[j * ATT_BLOCK:(j + 2) * ATT_BLOCK]
        vb = vext_ref[j * ATT_BLOCK:(j + 2) * ATT_BLOCK]
        qs = q_ref[0, j * ATT_BLOCK:(j + 1) * ATT_BLOCK, :]
        stack = []
        for r in range(N_HEADS):
            slab = qs[:, (r % 4) * LANES:(r % 4 + 1) * LANES]
            keep = low_half if r < 4 else jnp.logical_not(low_half)
            stack.append(jnp.where(keep, slab, zero_b))
        qst = jnp.concatenate(stack, axis=0)
        scores = lax.dot_general(qst, kb, (((1,), (1,)), ((), ())),
                                 preferred_element_type=F32)
        first = jnp.logical_and(i == 0, j == 0)
        pen = jnp.where(jnp.logical_and(first, col < ATT_BLOCK), NEG, 0.0)
        probs = []
        inv = []
        for r in range(N_HEADS):
            sr = scores[r * ATT_BLOCK:(r + 1) * ATT_BLOCK] + bias_ref[r] + pen
            sink = sink_ref[r]
            m = jnp.maximum(jnp.max(sr, axis=-1, keepdims=True), sink)
            e = jnp.exp(sr - m)
            den = jnp.sum(e, axis=-1, keepdims=True) + jnp.exp(sink - m)
            probs.append(e.astype(BF16))
            inv.append(1.0 / den)
        pst = jnp.concatenate(probs, axis=0)
        pv = jnp.dot(pst, vb, preferred_element_type=F32)
        for m4 in range(4):
            lo = pv[m4 * ATT_BLOCK:(m4 + 1) * ATT_BLOCK] * inv[m4]
            hi = pv[(m4 + 4) * ATT_BLOCK:(m4 + 5) * ATT_BLOCK] * inv[m4 + 4]
            attn_ref[j * ATT_BLOCK:(j + 1) * ATT_BLOCK, m4 * LANES:(m4 + 1) * LANES] = (
                jnp.where(low_half, lo, hi))

    ya = pool_ref[...]
    yb = attn_ref[...]
    ya = ya * lax.rsqrt(jnp.mean(ya * ya, axis=-1, keepdims=True) + RMS_EPS) * mg_ref[:, :POOL_WIDTH]
    yb = yb * lax.rsqrt(jnp.mean(yb * yb, axis=-1, keepdims=True) + RMS_EPS) * mg_ref[:, POOL_WIDTH:]
    out = (jnp.dot(ya.astype(BF16), wo_ref[:POOL_WIDTH, :], preferred_element_type=F32)
           + jnp.dot(yb.astype(BF16), wo_ref[POOL_WIDTH:, :], preferred_element_type=F32))
    o_ref[0] = x_ref[0] + mod_ref[0, 2:3, :] * out


def _mixer(x, p, q, k, v, mod_l, sinks, bias, pool_w_b, pool_scale, mix_g, w_out_b, ts):
    b, s, d = x.shape
    nb = ts // ATT_BLOCK
    nh = ts // POOL_HALO
    full = lambda shape: pl.BlockSpec(shape, lambda bi, i: (0,) * len(shape))
    row = lambda w: pl.BlockSpec((1, ts, w), lambda bi, i: (bi, i, 0))
    prev_blk = pl.BlockSpec((1, ATT_BLOCK, KV_WIDTH),
                            lambda bi, i: (bi, jnp.maximum(i * nb - 1, 0), 0))
    return pl.pallas_call(
        functools.partial(_mixer_kernel, ts=ts),
        grid=(b, s // ts),
        in_specs=[
            pl.BlockSpec(memory_space=pltpu.SMEM),
            row(d),
            row(POOL_WIDTH),
            pl.BlockSpec((1, POOL_HALO, POOL_WIDTH),
                         lambda bi, i: (bi, jnp.maximum(i * nh - 1, 0), 0)),
            row(ATTN_WIDTH),
            row(KV_WIDTH), prev_blk,
            row(KV_WIDTH), prev_blk,
            pl.BlockSpec((1, N_MOD, d), lambda bi, i: (bi, 0, 0)),
            full((N_HEADS, ATT_BLOCK, 2 * ATT_BLOCK)),
            full((len(POOL_WINDOWS), POOL_GROUP, POOL_GROUP)),
            full((1, POOL_WIDTH)),
            full((1, d)),
            full((d, d)),
        ],
        out_specs=row(d),
        out_shape=jax.ShapeDtypeStruct((b, s, d), F32),
        scratch_shapes=[
            pltpu.VMEM((ts + ATT_BLOCK, KV_WIDTH), BF16),
            pltpu.VMEM((ts + ATT_BLOCK, KV_WIDTH), BF16),
            pltpu.VMEM((ts, ATTN_WIDTH), F32),
            pltpu.VMEM((ts, POOL_WIDTH), F32),
        ],
        compiler_params=_cparams(("parallel", "parallel")),
        name="mixer",
    )(sinks, x, p, p, q, k, k, v, v, mod_l, bias, pool_w_b, pool_scale, mix_g, w_out_b)


def _oddeven_merge(lo, hi, r):
    step = r * 2
    if step < hi - lo:
        yield from _oddeven_merge(lo, hi, step)
        yield from _oddeven_merge(lo + r, hi, step)
        yield from [(i, i + r) for i in range(lo + r, hi - r, step)]
    else:
        yield (lo, lo + r)


def _oddeven_sort(lo, hi):
    if hi - lo >= 1:
        mid = lo + (hi - lo) // 2
        yield from _oddeven_sort(lo, mid)
        yield from _oddeven_sort(mid + 1, hi)
        yield from _oddeven_merge(lo, hi, 1)


_SORT16 = tuple(_oddeven_sort(0, PEER_TOPK - 1))
_BITONIC16 = tuple((i, i + st) for st in (8, 4, 2, 1) for i in range(PEER_TOPK) if not i & st)


def _compare_exchange(x, net):
    for i, j in net:
        if j < len(x):
            x[i], x[j] = jnp.maximum(x[i], x[j]), jnp.minimum(x[i], x[j])
    return x


def _top16_sorted(rows, smallest_only=False):
    x = _compare_exchange(list(rows), _SORT16)
    x = x + [jnp.full_like(x[0], NEG)] * (PEER_TOPK - len(x))
    for shift in (4, 2, 1):
        y = [pltpu.roll(v, shift, 0) for v in x]
        x = [jnp.maximum(x[v], y[PEER_TOPK - 1 - v]) for v in range(PEER_TOPK)]
        if smallest_only and shift == 1:
            return functools.reduce(jnp.minimum, x)
        x = _compare_exchange(x, _BITONIC16)
    return x


def _rank_among(x, tops):
    lt = [x < tops[7]]
    pick = lambda lo, hi, m: jnp.where(m, hi, lo)
    t4 = pick(tops[3], tops[11], lt[0])
    lt.append(x < t4)
    t2 = pick(pick(tops[1], tops[9], lt[0]), pick(tops[5], tops[13], lt[0]), lt[1])
    lt.append(x < t2)
    t1 = pick(pick(pick(tops[0], tops[8], lt[0]), pick(tops[4], tops[12], lt[0]), lt[1]),
              pick(pick(tops[2], tops[10], lt[0]), pick(tops[6], tops[14], lt[0]), lt[1]), lt[2])
    lt.append(x < t1)
    rank = functools.reduce(lambda r, mw: r + jnp.where(mw[0], mw[1], 0.0), zip(lt, (8.0, 4.0, 2.0, 1.0)), 0.0)
    return jnp.where(x < tops[PEER_TOPK - 1], float(PEER_TOPK), rank)


def _pack_rows(slabs, sub):
    out = slabs[-1]
    for k in range(len(slabs) - 2, -1, -1):
        out = jnp.where(sub == k, slabs[k], out)
    return out


def _route_kernel(x_ref, mod_ref, g_ref, wq_ref, keys_ref,
                  h_ref, r2_ref, e2_ref, n1_ref, w1_ref, s_ref, *, tt):
    x = x_ref[...]
    ms = jnp.mean(x * x, axis=-1, keepdims=True)
    y = x * lax.rsqrt(ms + RMS_EPS) * g_ref[...]
    h = (y * (1.0 + mod_ref[0, 4:5, :]) + mod_ref[0, 3:4, :]).astype(BF16)
    h_ref[...] = h
    q = jnp.dot(h, wq_ref[...], preferred_element_type=F32).astype(BF16)
    for hp in range(2 * PEER_HEADS):
        qhp = q[:, hp * D_HALF:(hp + 1) * D_HALF]
        s_ref[hp] = lax.dot_general(keys_ref[hp % 2], qhp, (((1,), (1,)), ((), ())),
                                    preferred_element_type=F32)

    sub = lax.broadcasted_iota(jnp.int32, (SUBLANES, LANES), 0)
    slabs = lambda m: [m[v * SUBLANES:(v + 1) * SUBLANES] for v in range(N_KEYS // SUBLANES)]
    one = lambda hit: jnp.where(hit, 1.0, 0.0)

    def per_chunk(idx, carry):
        head = idx // (tt // LANES)
        chunk = idx % (tt // LANES)
        cols = pl.ds(pl.multiple_of(chunk * LANES, LANES), LANES)
        s1 = s_ref[2 * head, :, cols]
        s2 = s_ref[2 * head + 1, :, cols]
        a = _top16_sorted(slabs(s1))
        bt = _top16_sorted(slabs(s2))
        a_lo, a_hi = _pack_rows(a[:SUBLANES], sub), _pack_rows(a[SUBLANES:], sub)
        b_hi = _pack_rows(bt[SUBLANES:], sub)
        cand = [a_lo + bt[0], a_hi + bt[0], a_lo + bt[1]]
        for r2 in range(2, SUBLANES):
            cand.append(jnp.where(sub < PEER_TOPK // (r2 + 1), a_lo + bt[r2], NEG))
        cand.append(b_hi + a[0])
        thr = _top16_sorted(cand, smallest_only=True)
        top = a[0] + bt[0]
        hits = [c >= thr for c in cand]
        z = None
        for c, hit in zip(cand, hits):
            zc = jnp.where(hit, jnp.exp(c - top), 0.0)
            z = zc if z is None else z + zc
        zinv = 0.5 / jnp.sum(z, axis=0, keepdims=True)
        cnt_lo = one(hits[0])
        for hit in hits[2:-1]:
            cnt_lo = cnt_lo + one(hit)
        tail = jnp.sum(one(hits[-1]), axis=0, keepdims=True)
        cnt_lo = cnt_lo + jnp.where(sub == 0, tail, 0.0)
        cnt_hi = one(hits[1])
        cnt = [jnp.broadcast_to((cnt_lo if r < SUBLANES else cnt_hi)[r % SUBLANES:r % SUBLANES + 1],
                                (SUBLANES, LANES)) for r in range(PEER_TOPK)]
        a0 = a[0][0:1]
        b0 = bt[0][0:1]
        for vp in range(N_KEYS // (2 * SUBLANES)):
            rk, ex = [], []
            for v in (2 * vp, 2 * vp + 1):
                rows = pl.ds(v * SUBLANES, SUBLANES)
                x1 = s1[v * SUBLANES:(v + 1) * SUBLANES]
                x2 = s2[v * SUBLANES:(v + 1) * SUBLANES]
                n1 = jnp.zeros_like(x1)
                for r in range(PEER_TOPK - 1, -1, -1):
                    n1 = jnp.where(x1 == a[r], cnt[r], n1)
                n1_ref[head, rows, cols] = n1
                w1_ref[head, rows, cols] = jnp.exp(x1 - a0) * zinv
                rk.append(_rank_among(x2, bt))
                ex.append(jnp.exp(x2 - b0))
            rows2 = pl.ds(vp * 2 * SUBLANES, 2 * SUBLANES)
            r2_ref[head, rows2, cols] = jnp.concatenate(rk, axis=0).astype(BF16)
            e2_ref[head, rows2, cols] = jnp.concatenate(ex, axis=0).astype(BF16)
        return carry

    lax.fori_loop(0, PEER_HEADS * (tt // LANES), per_chunk, 0)


def _route(x2, mod_l, norm_g, wq_b, keys_b, tt, s_len):
    t, d = x2.shape
    per_b = s_len // tt
    full = lambda shape: pl.BlockSpec(shape, lambda i: (0,) * len(shape))
    stat = lambda: pl.BlockSpec((PEER_HEADS, N_KEYS, tt), lambda i: (0, 0, i))
    return pl.pallas_call(
        functools.partial(_route_kernel, tt=tt),
        grid=(t // tt,),
        in_specs=[
            pl.BlockSpec((tt, d), lambda i: (i, 0)),
            pl.BlockSpec((1, N_MOD, d), lambda i: (i // per_b, 0, 0)),
            full((1, d)),
            full(wq_b.shape),
            full(keys_b.shape),
        ],
        out_specs=[pl.BlockSpec((tt, d), lambda i: (i, 0)), stat(), stat(), stat(), stat()],
        out_shape=[
            jax.ShapeDtypeStruct((t, d), BF16),
            jax.ShapeDtypeStruct((PEER_HEADS, N_KEYS, t), BF16),
            jax.ShapeDtypeStruct((PEER_HEADS, N_KEYS, t), BF16),
            jax.ShapeDtypeStruct((PEER_HEADS, N_KEYS, t), F32),
            jax.ShapeDtypeStruct((PEER_HEADS, N_KEYS, t), F32),
        ],
        scratch_shapes=[pltpu.VMEM((2 * PEER_HEADS, N_KEYS, tt), F32)],
        compiler_params=_cparams(("parallel",)),
        name="peer_route",
    )(x2, mod_l, norm_g, wq_b, keys_b)


def _peer_kernel(x_ref, h_ref, mod_ref, r2_ref, e2_ref, n1_ref, w1_ref, u_ref, vt_ref,
                 o_ref, acc_ref, act_ref, gate_ref, *, n_sub):
    e = pl.program_id(1)
    tt = h_ref.shape[0]

    @pl.when(e == 0)
    def _():
        acc_ref[...] = jnp.zeros_like(acc_ref)

    def row_tile(ref, hd, j):
        return jnp.broadcast_to(ref[hd, j:j + 1, :], (BF16_ROWS, tt)).astype(BF16)

    for j in range(n_sub):
        gate = None
        for hd in range(PEER_HEADS):
            n1 = row_tile(n1_ref, hd, j)
            w1 = row_tile(w1_ref, hd, j)
            term = jnp.where(r2_ref[hd] < n1[None], e2_ref[hd] * w1[None], jnp.zeros((), BF16))
            gate = term if gate is None else gate + term
        gate_ref[j * N_KEYS:(j + 1) * N_KEYS, :] = gate.reshape(N_KEYS, tt)
    a = lax.dot_general(u_ref[...], h_ref[...], (((1,), (1,)), ((), ())),
                        preferred_element_type=F32).astype(BF16)
    act_ref[...] = (a * gate_ref[...]) * (1.0 + lax.erf(a * (2.0 ** -0.5)))
    acc_ref[...] += jnp.dot(vt_ref[...], act_ref[...], preferred_element_type=F32)

    @pl.when(e == pl.num_programs(1) - 1)
    def _():
        o_ref[...] = x_ref[...] + mod_ref[0, 5:6, :] * acc_ref[...].T


def _peer(x2, h2, mod_l, r2, e2, n1, w1, u_b, vt_b, layer, tt, e_blk, s_len):
    t, d = x2.shape
    n_sub = e_blk // N_KEYS
    per_b = s_len // tt
    r2 = r2.reshape(PEER_HEADS, N_KEYS // BF16_ROWS, BF16_ROWS, t)
    e2 = e2.reshape(PEER_HEADS, N_KEYS // BF16_ROWS, BF16_ROWS, t)
    stat = lambda: pl.BlockSpec((PEER_HEADS, N_KEYS // BF16_ROWS, BF16_ROWS, tt), lambda i, e: (0, 0, 0, i))
    rows = lambda: pl.BlockSpec((PEER_HEADS, n_sub, tt), lambda i, e: (0, e, i))
    return pl.pallas_call(
        functools.partial(_peer_kernel, n_sub=n_sub),
        grid=(t // tt, N_EXPERTS // e_blk),
        in_specs=[
            pl.BlockSpec((tt, d), lambda i, e: (i, 0)),
            pl.BlockSpec((tt, d), lambda i, e: (i, 0)),
            pl.BlockSpec((1, N_MOD, d), lambda i, e: (i // per_b, 0, 0)),
            stat(), stat(), rows(), rows(),
            pl.BlockSpec((None, e_blk, d), lambda i, e: (layer, e, 0)),
            pl.BlockSpec((None, d, e_blk), lambda i, e: (layer, 0, e)),
        ],
        out_specs=pl.BlockSpec((tt, d), lambda i, e: (i, 0)),
        out_shape=jax.ShapeDtypeStruct((t, d), F32),
        scratch_shapes=[pltpu.VMEM((d, tt), F32), pltpu.VMEM((e_blk, tt), BF16),
                        pltpu.VMEM((e_blk, tt), BF16)],
        compiler_params=_cparams(("parallel", "arbitrary")),
        name="peer_dense",
    )(x2, h2, mod_l, r2, e2, n1, w1, u_b, vt_b)


def _vt_kernel(v_ref, o_ref):
    o_ref[0] = v_ref[0].T.astype(BF16)


def _transpose_values(peer_v, te=1024):
    depth, n_e, d = peer_v.shape
    return pl.pallas_call(
        _vt_kernel,
        grid=(depth, n_e // te),
        in_specs=[pl.BlockSpec((1, te, d), lambda l, i: (l, i, 0))],
        out_specs=pl.BlockSpec((1, d, te), lambda l, i: (l, 0, i)),
        out_shape=jax.ShapeDtypeStruct((depth, d, n_e), BF16),
        compiler_params=_cparams(("parallel", "parallel")),
        name="v_transpose",
    )(peer_v)


def _pair_heads(w, axis):
    shape = w.shape
    w = w.reshape(shape[:axis] + (2, N_HEADS // 2, HEAD_DIM) + shape[axis + 1:])
    return jnp.swapaxes(w, axis, axis + 1).reshape(shape)


def _attn_bias():
    qi = jnp.arange(ATT_BLOCK)[:, None]
    sj = jnp.arange(2 * ATT_BLOCK)[None, :]
    dist = qi + ATT_BLOCK - sj
    ok = (dist >= 0) & (dist < ATT_BLOCK)
    slopes = jnp.exp2(-8.0 * jnp.arange(1, N_HEADS + 1, dtype=F32) / N_HEADS)
    bias = -slopes[:, None, None] * dist.astype(F32)[None]
    return jnp.where(ok[None], bias, NEG)


def kernel(x, c, w_ada, b_ada, norm1_g, norm2_g, w_in, pool_w, pool_scale, q_norm_g, k_norm_g,
           attn_sinks, mix_norm_g, w_out, peer_wq, peer_subkeys, peer_u, peer_v):
    b, s, d = x.shape
    depth = w_ada.shape[0]
    t = b * s
    ts = min(1024, s)
    tt_route = min(512, s)
    tt_peer = min(1024, s)
    e_blk = 1024

    q0, k0 = POOL_WIDTH, POOL_WIDTH + ATTN_WIDTH
    w_in_b = jnp.concatenate([w_in[:, :, :q0], _pair_heads(w_in[:, :, q0:k0], 2), w_in[:, :, k0:]],
                             axis=2).astype(BF16)
    w_out_b = jnp.concatenate([w_out[:, :q0], _pair_heads(w_out[:, q0:], 1)], axis=1).astype(BF16)
    mix_g = jnp.concatenate([mix_norm_g[:, :q0], _pair_heads(mix_norm_g[:, q0:], 1)], axis=1)
    pool_w_b = pool_w.astype(BF16)
    wq_b = peer_wq.astype(BF16)
    keys_b = peer_subkeys.astype(BF16)
    u_b = peer_u.astype(BF16)
    vt_b = _transpose_values(peer_v)
    qg_t = jnp.tile(q_norm_g, (1, N_HEADS))
    kg_t = jnp.tile(k_norm_g, (1, N_KV_HEADS))
    lane_head = jnp.arange(ATTN_WIDTH) // HEAD_DIM
    bd = jnp.where(lane_head[:, None] == lane_head[None, :], 1.0 / HEAD_DIM, 0.0).astype(BF16)
    bias = _attn_bias()

    mod = _adaln_mod(c, w_ada, b_ada).reshape(depth, b, N_MOD, d)

    for l in range(depth):
        p, q, k, v = _inproj(x, mod[l], norm1_g[l:l + 1], w_in_b[l], qg_t[l:l + 1], kg_t[l:l + 1], bd, ts)
        x = _mixer(x, p, q, k, v, mod[l], attn_sinks[l], bias, pool_w_b[l], pool_scale[l:l + 1],
                   mix_g[l:l + 1], w_out_b[l], ts)
        x2 = x.reshape(t, d)
        h2, r2, e2, n1, w1 = _route(x2, mod[l], norm2_g[l:l + 1], wq_b[l], keys_b[l], tt_route, s)
        x = _peer(x2, h2, mod[l], r2, e2, n1, w1, u_b, vt_b, l, tt_peer, e_blk, s).reshape(b, s, d)
    return x
```

```python
import functools

import jax
import jax.numpy as jnp
from jax import lax
from jax.experimental import pallas as pl
from jax.experimental.pallas import tpu as pltpu

F32 = jnp.float32
BF16 = jnp.bfloat16

D_MODEL = 1024
POOL_WIDTH = 512
POOL_WINDOWS = (2, 4, 8, 16)
POOL_GROUP = 128
POOL_HALO = 16
ATTN_WIDTH = 512
HEAD_DIM = 64
N_HEADS = 8
N_KV_HEADS = 2
KV_WIDTH = N_KV_HEADS * HEAD_DIM
ATT_BLOCK = 128
N_KEYS = 128
N_EXPERTS = N_KEYS * N_KEYS
PEER_HEADS = 8
PEER_TOPK = 16
D_HALF = 128
RMS_EPS = 1e-6
N_MOD = 6
NEG = -1e30

LANES = 128
SUBLANES = 8
BF16_ROWS = 2 * SUBLANES
VMEM_LIMIT = 56 * 1024 * 1024


def _cparams(sem):
    return pltpu.CompilerParams(dimension_semantics=sem, vmem_limit_bytes=VMEM_LIMIT)


def _mod_kernel(c_ref, w_ref, b_ref, o_ref):
    c = c_ref[...]
    cond = c * jax.nn.sigmoid(c)
    o_ref[0] = jnp.dot(cond, w_ref[0], preferred_element_type=F32,
                       precision=lax.Precision.HIGHEST) + b_ref[0]


def _adaln_mod(c, w_ada, b_ada, tn=3072):
    depth, d, n = w_ada.shape
    b = c.shape[0]
    return pl.pallas_call(
        _mod_kernel,
        grid=(depth, n // tn),
        in_specs=[
            pl.BlockSpec((b, d), lambda l, j: (0, 0)),
            pl.BlockSpec((1, d, tn), lambda l, j: (l, 0, j)),
            pl.BlockSpec((1, 1, tn), lambda l, j: (l, 0, j)),
        ],
        out_specs=pl.BlockSpec((1, b, tn), lambda l, j: (l, 0, j)),
        out_shape=jax.ShapeDtypeStruct((depth, b, n), F32),
        compiler_params=_cparams(("parallel", "parallel")),
        name="adaln_mod",
    )(c, w_ada, b_ada.reshape(depth, 1, n))


def _head_mean(t, bd):
    return jnp.dot(t.astype(BF16), bd, preferred_element_type=F32)


def _project(x, mod_ref, g_ref, w_ref, qg_ref, kg_ref, bd_ref):
    ms = jnp.mean(x * x, axis=-1, keepdims=True)
    y = x * lax.rsqrt(ms + RMS_EPS) * g_ref[...]
    h = y * (1.0 + mod_ref[0, 1:2, :]) + mod_ref[0, 0:1, :]
    proj = jnp.dot(h.astype(BF16), w_ref[...], preferred_element_type=F32)
    q0, k0, v0 = POOL_WIDTH, POOL_WIDTH + ATTN_WIDTH, POOL_WIDTH + ATTN_WIDTH + KV_WIDTH
    q = proj[:, q0:k0]
    k = proj[:, k0:v0]
    bd = bd_ref[...]
    q_ms = _head_mean(q * q, bd)
    k_ms = _head_mean(k * k, bd[:KV_WIDTH, :KV_WIDTH])
    qn = q * lax.rsqrt(q_ms + RMS_EPS) * qg_ref[...]
    kn = k * lax.rsqrt(k_ms + RMS_EPS) * kg_ref[...]
    return proj[:, :q0], (qn * (HEAD_DIM ** -0.5)).astype(BF16), kn.astype(BF16), proj[:, v0:].astype(BF16)


def _mixer_kernel(sink_ref, x_ref, mod_ref, g_ref, w_ref, qg_ref, kg_ref, bd_ref,
                  bias_ref, pw_ref, ps_ref, mg_ref, wo_ref, o_ref,
                  pext_ref, q_ref, kext_ref, vext_ref, attn_ref, pool_ref, *, ts):
    i = pl.program_id(1)
    nblk = ts // ATT_BLOCK

    @pl.when(i == 0)
    def _():
        pext_ref[0:POOL_HALO] = jnp.zeros((POOL_HALO, POOL_WIDTH), F32)
        kext_ref[0:ATT_BLOCK] = jnp.zeros((ATT_BLOCK, KV_WIDTH), BF16)
        vext_ref[0:ATT_BLOCK] = jnp.zeros((ATT_BLOCK, KV_WIDTH), BF16)

    p_new, q_new, k_new, v_new = _project(x_ref[0], mod_ref, g_ref, w_ref, qg_ref, kg_ref, bd_ref)
    pext_ref[POOL_HALO:] = p_new
    q_ref[...] = q_new
    kext_ref[ATT_BLOCK:] = k_new
    vext_ref[ATT_BLOCK:] = v_new

    pext = pext_ref[...]
    pos1 = (lax.broadcasted_iota(jnp.int32, (ts, 1), 0) + i * ts + 1).astype(F32)
    for g, w in enumerate(POOL_WINDOWS):
        xg = pext[:, g * POOL_GROUP:(g + 1) * POOL_GROUP]
        sg = xg
        step = 1
        while step < w:
            sg = sg + pltpu.roll(sg, step, 0)
            step *= 2
        pooled = sg[POOL_HALO:] / jnp.minimum(pos1, float(w)) - xg[POOL_HALO:]
        mixed = jnp.dot(pooled.astype(BF16), pw_ref[g], preferred_element_type=F32)
        pool_ref[:, g * POOL_GROUP:(g + 1) * POOL_GROUP] = (
            mixed * ps_ref[:, g * POOL_GROUP:(g + 1) * POOL_GROUP])

    lane = lax.broadcasted_iota(jnp.int32, (ATT_BLOCK, LANES), 1)
    low_half = lane < HEAD_DIM
    col = lax.broadcasted_iota(jnp.int32, (1, 2 * ATT_BLOCK), 1)
    zero_b = jnp.zeros((ATT_BLOCK, LANES), BF16)
    for j in range(nblk):
        kb = kext_ref[j * ATT_BLOCK:(j + 2) * ATT_BLOCK]
        vb = vext_ref[j * ATT_BLOCK:(j + 2) * ATT_BLOCK]
        qs = q_ref[j * ATT_BLOCK:(j + 1) * ATT_BLOCK, :]
        stack = []
        for r in range(N_HEADS):
            slab = qs[:, (r % 4) * LANES:(r % 4 + 1) * LANES]
            keep = low_half if r < 4 else jnp.logical_not(low_half)
            stack.append(jnp.where(keep, slab, zero_b))
        qst = jnp.concatenate(stack, axis=0)
        scores = lax.dot_general(qst, kb, (((1,), (1,)), ((), ())),
                                 preferred_element_type=F32)
        first = jnp.logical_and(i == 0, j == 0)
        pen = jnp.where(jnp.logical_and(first, col < ATT_BLOCK), NEG, 0.0)
        probs = []
        inv = []
        for r in range(N_HEADS):
            sr = scores[r * ATT_BLOCK:(r + 1) * ATT_BLOCK] + bias_ref[r] + pen
            sink = sink_ref[r]
            m = jnp.maximum(jnp.max(sr, axis=-1, keepdims=True), sink)
            e = jnp.exp(sr - m)
            den = jnp.sum(e, axis=-1, keepdims=True) + jnp.exp(sink - m)
            probs.append(e.astype(BF16))
            inv.append(1.0 / den)
        pst = jnp.concatenate(probs, axis=0)
        pv = jnp.dot(pst, vb, preferred_element_type=F32)
        for m4 in range(4):
            lo = pv[m4 * ATT_BLOCK:(m4 + 1) * ATT_BLOCK] * inv[m4]
            hi = pv[(m4 + 4) * ATT_BLOCK:(m4 + 5) * ATT_BLOCK] * inv[m4 + 4]
            attn_ref[j * ATT_BLOCK:(j + 1) * ATT_BLOCK, m4 * LANES:(m4 + 1) * LANES] = (
                jnp.where(low_half, lo, hi))

    ya = pool_ref[...]
    yb = attn_ref[...]
    ya = ya * lax.rsqrt(jnp.mean(ya * ya, axis=-1, keepdims=True) + RMS_EPS) * mg_ref[:, :POOL_WIDTH]
    yb = yb * lax.rsqrt(jnp.mean(yb * yb, axis=-1, keepdims=True) + RMS_EPS) * mg_ref[:, POOL_WIDTH:]
    out = (jnp.dot(ya.astype(BF16), wo_ref[:POOL_WIDTH, :], preferred_element_type=F32)
           + jnp.dot(yb.astype(BF16), wo_ref[POOL_WIDTH:, :], preferred_element_type=F32))
    o_ref[0] = x_ref[0] + mod_ref[0, 2:3, :] * out

    pext_ref[0:POOL_HALO] = pext_ref[ts:ts + POOL_HALO]
    kext_ref[0:ATT_BLOCK] = kext_ref[ts:ts + ATT_BLOCK]
    vext_ref[0:ATT_BLOCK] = vext_ref[ts:ts + ATT_BLOCK]


def _mixer(x, mod_l, norm_g, w_in_b, qg_t, kg_t, bd, sinks, bias, pool_w_b, pool_scale, mix_g, w_out_b, ts):
    b, s, d = x.shape
    full = lambda shape: pl.BlockSpec(shape, lambda bi, i: (0,) * len(shape))
    row = lambda w: pl.BlockSpec((1, ts, w), lambda bi, i: (bi, i, 0))
    return pl.pallas_call(
        functools.partial(_mixer_kernel, ts=ts),
        grid=(b, s // ts),
        in_specs=[
            pl.BlockSpec(memory_space=pltpu.SMEM),
            row(d),
            pl.BlockSpec((1, N_MOD, d), lambda bi, i: (bi, 0, 0)),
            full((1, d)),
            full(w_in_b.shape),
            full((1, ATTN_WIDTH)),
            full((1, KV_WIDTH)),
            full((ATTN_WIDTH, ATTN_WIDTH)),
            full((N_HEADS, ATT_BLOCK, 2 * ATT_BLOCK)),
            full((len(POOL_WINDOWS), POOL_GROUP, POOL_GROUP)),
            full((1, POOL_WIDTH)),
            full((1, d)),
            full((d, d)),
        ],
        out_specs=row(d),
        out_shape=jax.ShapeDtypeStruct((b, s, d), F32),
        scratch_shapes=[
            pltpu.VMEM((POOL_HALO + ts, POOL_WIDTH), F32),
            pltpu.VMEM((ts, ATTN_WIDTH), BF16),
            pltpu.VMEM((ts + ATT_BLOCK, KV_WIDTH), BF16),
            pltpu.VMEM((ts + ATT_BLOCK, KV_WIDTH), BF16),
            pltpu.VMEM((ts, ATTN_WIDTH), F32),
            pltpu.VMEM((ts, POOL_WIDTH), F32),
        ],
        compiler_params=_cparams(("parallel", "arbitrary")),
        name="mixer",
    )(sinks, x, mod_l, norm_g, w_in_b, qg_t, kg_t, bd, bias, pool_w_b, pool_scale, mix_g, w_out_b)


def _oddeven_merge(lo, hi, r):
    step = r * 2
    if step < hi - lo:
        yield from _oddeven_merge(lo, hi, step)
        yield from _oddeven_merge(lo + r, hi, step)
        yield from [(i, i + r) for i in range(lo + r, hi - r, step)]
    else:
        yield (lo, lo + r)


def _oddeven_sort(lo, hi):
    if hi - lo >= 1:
        mid = lo + (hi - lo) // 2
        yield from _oddeven_sort(lo, mid)
        yield from _oddeven_sort(mid + 1, hi)
        yield from _oddeven_merge(lo, hi, 1)


_SORT16 = tuple(_oddeven_sort(0, PEER_TOPK - 1))
_BITONIC16 = tuple((i, i + st) for st in (8, 4, 2, 1) for i in range(PEER_TOPK) if not i & st)


def _compare_exchange(x, net):
    for i, j in net:
        if j < len(x):
            x[i], x[j] = jnp.maximum(x[i], x[j]), jnp.minimum(x[i], x[j])
    return x


def _top16_sorted(rows, smallest_only=False):
    x = _compare_exchange(list(rows), _SORT16)
    x = x + [jnp.full_like(x[0], NEG)] * (PEER_TOPK - len(x))
    for shift in (4, 2, 1):
        y = [pltpu.roll(v, shift, 0) for v in x]
        x = [jnp.maximum(x[v], y[PEER_TOPK - 1 - v]) for v in range(PEER_TOPK)]
        if smallest_only and shift == 1:
            return functools.reduce(jnp.minimum, x)
        x = _compare_exchange(x, _BITONIC16)
    return x


def _rank_among(x, tops):
    lt = [x < tops[7]]
    pick = lambda lo, hi, m: jnp.where(m, hi, lo)
    t4 = pick(tops[3], tops[11], lt[0])
    lt.append(x < t4)
    t2 = pick(pick(tops[1], tops[9], lt[0]), pick(tops[5], tops[13], lt[0]), lt[1])
    lt.append(x < t2)
    t1 = pick(pick(pick(tops[0], tops[8], lt[0]), pick(tops[4], tops[12], lt[0]), lt[1]),
              pick(pick(tops[2], tops[10], lt[0]), pick(tops[6], tops[14], lt[0]), lt[1]), lt[2])
    lt.append(x < t1)
    rank = functools.reduce(lambda r, mw: r + jnp.where(mw[0], mw[1], 0.0), zip(lt, (8.0, 4.0, 2.0, 1.0)), 0.0)
    return jnp.where(x < tops[PEER_TOPK - 1], float(PEER_TOPK), rank)


def _pack_rows(slabs, sub):
    out = slabs[-1]
    for k in range(len(slabs) - 2, -1, -1):
        out = jnp.where(sub == k, slabs[k], out)
    return out


def _route_kernel(x_ref, mod_ref, g_ref, wq_ref, keys_ref,
                  h_ref, r2_ref, e2_ref, n1_ref, w1_ref, s_ref, *, tt):
    x = x_ref[...]
    ms = jnp.mean(x * x, axis=-1, keepdims=True)
    y = x * lax.rsqrt(ms + RMS_EPS) * g_ref[...]
    h = (y * (1.0 + mod_ref[0, 4:5, :]) + mod_ref[0, 3:4, :]).astype(BF16)
    h_ref[...] = h
    q = jnp.dot(h, wq_ref[...], preferred_element_type=F32).astype(BF16)
    for hp in range(2 * PEER_HEADS):
        qhp = q[:, hp * D_HALF:(hp + 1) * D_HALF]
        s_ref[hp] = lax.dot_general(keys_ref[hp % 2], qhp, (((1,), (1,)), ((), ())),
                                    preferred_element_type=F32)

    sub = lax.broadcasted_iota(jnp.int32, (SUBLANES, LANES), 0)
    slabs = lambda m: [m[v * SUBLANES:(v + 1) * SUBLANES] for v in range(N_KEYS // SUBLANES)]
    one = lambda hit: jnp.where(hit, 1.0, 0.0)

    def per_chunk(idx, carry):
        head = idx // (tt // LANES)
        chunk = idx % (tt // LANES)
        cols = pl.ds(pl.multiple_of(chunk * LANES, LANES), LANES)
        s1 = s_ref[2 * head, :, cols]
        s2 = s_ref[2 * head + 1, :, cols]
        a = _top16_sorted(slabs(s1))
        bt = _top16_sorted(slabs(s2))
        a_lo, a_hi = _pack_rows(a[:SUBLANES], sub), _pack_rows(a[SUBLANES:], sub)
        b_hi = _pack_rows(bt[SUBLANES:], sub)
        cand = [a_lo + bt[0], a_hi + bt[0], a_lo + bt[1]]
        for r2 in range(2, SUBLANES):
            cand.append(jnp.where(sub < PEER_TOPK // (r2 + 1), a_lo + bt[r2], NEG))
        cand.append(b_hi + a[0])
        thr = _top16_sorted(cand, smallest_only=True)
        top = a[0] + bt[0]
        hits = [c >= thr for c in cand]
        z = None
        for c, hit in zip(cand, hits):
            zc = jnp.where(hit, jnp.exp(c - top), 0.0)
            z = zc if z is None else z + zc
        zinv = 0.5 / jnp.sum(z, axis=0, keepdims=True)
        cnt_lo = one(hits[0])
        for hit in hits[2:-1]:
            cnt_lo = cnt_lo + one(hit)
        tail = jnp.sum(one(hits[-1]), axis=0, keepdims=True)
        cnt_lo = cnt_lo + jnp.where(sub == 0, tail, 0.0)
        cnt_hi = one(hits[1])
        cnt = [jnp.broadcast_to((cnt_lo if r < SUBLANES else cnt_hi)[r % SUBLANES:r % SUBLANES + 1],
                                (SUBLANES, LANES)) for r in range(PEER_TOPK)]
        a0 = a[0][0:1]
        b0 = bt[0][0:1]
        for vp in range(N_KEYS // (2 * SUBLANES)):
            rk, ex = [], []
            for v in (2 * vp, 2 * vp + 1):
                rows = pl.ds(v * SUBLANES, SUBLANES)
                x1 = s1[v * SUBLANES:(v + 1) * SUBLANES]
                x2 = s2[v * SUBLANES:(v + 1) * SUBLANES]
                n1 = jnp.zeros_like(x1)
                for r in range(PEER_TOPK - 1, -1, -1):
                    n1 = jnp.where(x1 == a[r], cnt[r], n1)
                n1_ref[head, rows, cols] = n1
                w1_ref[head, rows, cols] = jnp.exp(x1 - a0) * zinv
                rk.append(_rank_among(x2, bt))
                ex.append(jnp.exp(x2 - b0))
            rows2 = pl.ds(vp * 2 * SUBLANES, 2 * SUBLANES)
            r2_ref[head, rows2, cols] = jnp.concatenate(rk, axis=0).astype(BF16)
            e2_ref[head, rows2, cols] = jnp.concatenate(ex, axis=0).astype(BF16)
        return carry

    lax.fori_loop(0, PEER_HEADS * (tt // LANES), per_chunk, 0)


def _route(x2, mod_l, norm_g, wq_b, keys_b, tt, s_len):
    t, d = x2.shape
    per_b = s_len // tt
    full = lambda shape: pl.BlockSpec(shape, lambda i: (0,) * len(shape))
    stat = lambda: pl.BlockSpec((PEER_HEADS, N_KEYS, tt), lambda i: (0, 0, i))
    return pl.pallas_call(
        functools.partial(_route_kernel, tt=tt),
        grid=(t // tt,),
        in_specs=[
            pl.BlockSpec((tt, d), lambda i: (i, 0)),
            pl.BlockSpec((1, N_MOD, d), lambda i: (i // per_b, 0, 0)),
            full((1, d)),
            full(wq_b.shape),
            full(keys_b.shape),
        ],
        out_specs=[pl.BlockSpec((tt, d), lambda i: (i, 0)), stat(), stat(), stat(), stat()],
        out_shape=[
            jax.ShapeDtypeStruct((t, d), BF16),
            jax.ShapeDtypeStruct((PEER_HEADS, N_KEYS, t), BF16),
            jax.ShapeDtypeStruct((PEER_HEADS, N_KEYS, t), BF16),
            jax.ShapeDtypeStruct((PEER_HEADS, N_KEYS, t), F32),
            jax.ShapeDtypeStruct((PEER_HEADS, N_KEYS, t), F32),
        ],
        scratch_shapes=[pltpu.VMEM((2 * PEER_HEADS, N_KEYS, tt), F32)],
        compiler_params=_cparams(("parallel",)),
        name="peer_route",
    )(x2, mod_l, norm_g, wq_b, keys_b)


def _peer_kernel(x_ref, h_ref, mod_ref, r2_ref, e2_ref, n1_ref, w1_ref, u_ref, vt_ref,
                 o_ref, acc_ref, act_ref, gate_ref, *, n_sub):
    e = pl.program_id(1)
    tt = h_ref.shape[0]

    @pl.when(e == 0)
    def _():
        acc_ref[...] = jnp.zeros_like(acc_ref)

    def row_tile(ref, hd, j):
        return jnp.broadcast_to(ref[hd, j:j + 1, :], (BF16_ROWS, tt)).astype(BF16)

    for j in range(n_sub):
        gate = None
        for hd in range(PEER_HEADS):
            n1 = row_tile(n1_ref, hd, j)
            w1 = row_tile(w1_ref, hd, j)
            term = jnp.where(r2_ref[hd] < n1[None], e2_ref[hd] * w1[None], jnp.zeros((), BF16))
            gate = term if gate is None else gate + term
        gate_ref[j * N_KEYS:(j + 1) * N_KEYS, :] = gate.reshape(N_KEYS, tt)
    a = lax.dot_general(u_ref[...], h_ref[...], (((1,), (1,)), ((), ())),
                        preferred_element_type=F32).astype(BF16)
    act_ref[...] = (a * gate_ref[...]) * (1.0 + lax.erf(a * (2.0 ** -0.5)))
    acc_ref[...] += jnp.dot(vt_ref[...], act_ref[...], preferred_element_type=F32)

    @pl.when(e == pl.num_programs(1) - 1)
    def _():
        o_ref[...] = x_ref[...] + mod_ref[0, 5:6, :] * acc_ref[...].T


def _peer(x2, h2, mod_l, r2, e2, n1, w1, u_b, vt_b, layer, tt, e_blk, s_len):
    t, d = x2.shape
    n_sub = e_blk // N_KEYS
    per_b = s_len // tt
    r2 = r2.reshape(PEER_HEADS, N_KEYS // BF16_ROWS, BF16_ROWS, t)
    e2 = e2.reshape(PEER_HEADS, N_KEYS // BF16_ROWS, BF16_ROWS, t)
    stat = lambda: pl.BlockSpec((PEER_HEADS, N_KEYS // BF16_ROWS, BF16_ROWS, tt), lambda i, e: (0, 0, 0, i))
    rows = lambda: pl.BlockSpec((PEER_HEADS, n_sub, tt), lambda i, e: (0, e, i))
    return pl.pallas_call(
        functools.partial(_peer_kernel, n_sub=n_sub),
        grid=(t // tt, N_EXPERTS // e_blk),
        in_specs=[
            pl.BlockSpec((tt, d), lambda i, e: (i, 0)),
            pl.BlockSpec((tt, d), lambda i, e: (i, 0)),
            pl.BlockSpec((1, N_MOD, d), lambda i, e: (i // per_b, 0, 0)),
            stat(), stat(), rows(), rows(),
            pl.BlockSpec((None, e_blk, d), lambda i, e: (layer, e, 0)),
            pl.BlockSpec((None, d, e_blk), lambda i, e: (layer, 0, e)),
        ],
        out_specs=pl.BlockSpec((tt, d), lambda i, e: (i, 0)),
        out_shape=jax.ShapeDtypeStruct((t, d), F32),
        scratch_shapes=[pltpu.VMEM((d, tt), F32), pltpu.VMEM((e_blk, tt), BF16),
                        pltpu.VMEM((e_blk, tt), BF16)],
        compiler_params=_cparams(("parallel", "arbitrary")),
        name="peer_dense",
    )(x2, h2, mod_l, r2, e2, n1, w1, u_b, vt_b)


def _vt_kernel(v_ref, o_ref):
    o_ref[0] = v_ref[0].T.astype(BF16)


def _transpose_values(peer_v, te=1024):
    depth, n_e, d = peer_v.shape
    return pl.pallas_call(
        _vt_kernel,
        grid=(depth, n_e // te),
        in_specs=[pl.BlockSpec((1, te, d), lambda l, i: (l, i, 0))],
        out_specs=pl.BlockSpec((1, d, te), lambda l, i: (l, 0, i)),
        out_shape=jax.ShapeDtypeStruct((depth, d, n_e), BF16),
        compiler_params=_cparams(("parallel", "parallel")),
        name="v_transpose",
    )(peer_v)


def _pair_heads(w, axis):
    shape = w.shape
    w = w.reshape(shape[:axis] + (2, N_HEADS // 2, HEAD_DIM) + shape[axis + 1:])
    return jnp.swapaxes(w, axis, axis + 1).reshape(shape)


def _attn_bias():
    qi = jnp.arange(ATT_BLOCK)[:, None]
    sj = jnp.arange(2 * ATT_BLOCK)[None, :]
    dist = qi + ATT_BLOCK - sj
    ok = (dist >= 0) & (dist < ATT_BLOCK)
    slopes = jnp.exp2(-8.0 * jnp.arange(1, N_HEADS + 1, dtype=F32) / N_HEADS)
    bias = -slopes[:, None, None] * dist.astype(F32)[None]
    return jnp.where(ok[None], bias, NEG)


def kernel(x, c, w_ada, b_ada, norm1_g, norm2_g, w_in, pool_w, pool_scale, q_norm_g, k_norm_g,
           attn_sinks, mix_norm_g, w_out, peer_wq, peer_subkeys, peer_u, peer_v):
    b, s, d = x.shape
    depth = w_ada.shape[0]
    t = b * s
    ts = min(1024, s)
    tt_route = min(512, s)
    tt_peer = min(1024, s)
    e_blk = 1024

    q0, k0 = POOL_WIDTH, POOL_WIDTH + ATTN_WIDTH
    w_in_b = jnp.concatenate([w_in[:, :, :q0], _pair_heads(w_in[:, :, q0:k0], 2), w_in[:, :, k0:]],
                             axis=2).astype(BF16)
    w_out_b = jnp.concatenate([w_out[:, :q0], _pair_heads(w_out[:, q0:], 1)], axis=1).astype(BF16)
    mix_g = jnp.concatenate([mix_norm_g[:, :q0], _pair_heads(mix_norm_g[:, q0:], 1)], axis=1)
    pool_w_b = pool_w.astype(BF16)
    wq_b = peer_wq.astype(BF16)
    keys_b = peer_subkeys.astype(BF16)
    u_b = peer_u.astype(BF16)
    vt_b = _transpose_values(peer_v)
    qg_t = jnp.tile(q_norm_g, (1, N_HEADS))
    kg_t = jnp.tile(k_norm_g, (1, N_KV_HEADS))
    lane_head = jnp.arange(ATTN_WIDTH) // HEAD_DIM
    bd = jnp.where(lane_head[:, None] == lane_head[None, :], 1.0 / HEAD_DIM, 0.0).astype(BF16)
    bias = _attn_bias()

    mod = _adaln_mod(c, w_ada, b_ada).reshape(depth, b, N_MOD, d)

    for l in range(depth):
        x = _mixer(x, mod[l], norm1_g[l:l + 1], w_in_b[l], qg_t[l:l + 1], kg_t[l:l + 1], bd, attn_sinks[l], bias,
                   pool_w_b[l], pool_scale[l:l + 1], mix_g[l:l + 1], w_out_b[l], ts)
        x2 = x.reshape(t, d)
        h2, r2, e2, n1, w1 = _route(x2, mod[l], norm2_g[l:l + 1], wq_b[l], keys_b[l], tt_route, s)
        x = _peer(x2, h2, mod[l], r2, e2, n1, w1, u_b, vt_b, l, tt_peer, e_blk, s).reshape(b, s, d)
    return x
```
